```python
import math
import jax, jax.numpy as jnp
from jax import lax
import numpy as np

D_MODEL = 2048
BATCH = 2
SEQ = 8192
DEPTH = 2

CHUNK = 64
QBLOCK = 128
N_MIXERS = 2
FOX_HEADS = 16
FOX_HEAD_DIM = D_MODEL // FOX_HEADS
RET_HEADS = 8
RET_QK_DIM = D_MODEL // RET_HEADS
RET_QK_WIDTH = RET_HEADS * RET_QK_DIM
RET_V_WIDTH = 2 * D_MODEL
RET_V_DIM = RET_V_WIDTH // RET_HEADS
ROPE_BASE = 10000.0
PEER_HEADS = 8
PEER_N_KEYS = 128
PEER_N_EXPERTS = PEER_N_KEYS * PEER_N_KEYS
PEER_QUERY_DIM = 256
PEER_HALF = PEER_QUERY_DIM // 2
PEER_TOPK = 16
PEER_TOKEN_BLOCK = 128
DN_ALPHA = (2 * DEPTH) ** 0.25
DN_BETA = (8 * DEPTH) ** -0.25
LN_EPS = 1e-5

kernel_name = "fox_retnet_peer_deepnorm_hybrid"


def layer_norm(x, g, b):
    xf = x.astype(jnp.float32)
    mu = jnp.mean(xf, axis=-1, keepdims=True)
    var = jnp.mean(jnp.square(xf - mu), axis=-1, keepdims=True)
    y = (xf - mu) * lax.rsqrt(var + LN_EPS)
    return (y * g + b).astype(x.dtype)


def rope(a, pos):
    half = a.shape[-1] // 2
    inv_freq = ROPE_BASE ** (-jnp.arange(half, dtype=jnp.float32) / half)
    ang = pos[:, None] * inv_freq[None, :]
    cos = jnp.cos(ang)[None, :, None, :]
    sin = jnp.sin(ang)[None, :, None, :]
    a1, a2 = a[..., :half], a[..., half:]
    return jnp.concatenate([a1 * cos - a2 * sin, a1 * sin + a2 * cos], axis=-1)


def fox_mixer(x, w_in, b_f, w_o):
    B, S, D = x.shape
    proj = x @ w_in
    q = proj[..., :D].reshape(B, S, FOX_HEADS, FOX_HEAD_DIM) * (FOX_HEAD_DIM ** -0.5)
    k = proj[..., D:2 * D].reshape(B, S, FOX_HEADS, FOX_HEAD_DIM)
    v = proj[..., 2 * D:3 * D].reshape(B, S, FOX_HEADS, FOX_HEAD_DIM)
    log_f = jax.nn.log_sigmoid((proj[..., 3 * D:] + b_f).astype(jnp.float32))
    c = jnp.cumsum(log_f, axis=1).transpose(0, 2, 1)
    outs = []
    for blk in range(S // QBLOCK):
        lo, hi = blk * QBLOCK, (blk + 1) * QBLOCK
        logits = jnp.einsum('bqhd,bkhd->bhqk', q[:, lo:hi], k[:, :hi]).astype(jnp.float32)
        logits = logits + c[:, :, lo:hi, None] - c[:, :, None, :hi]
        mask = jnp.arange(lo, hi)[:, None] >= jnp.arange(hi)[None, :]
        logits = jnp.where(mask, logits, -jnp.inf)
        p = jax.nn.softmax(logits, axis=-1).astype(v.dtype)
        outs.append(jnp.einsum('bhqk,bkhd->bqhd', p, v[:, :hi]))
    o = jnp.concatenate(outs, axis=1).reshape(B, S, D)
    return o @ w_o


def retention_mixer(x, w_in, gn_g, w_o):
    B, S, D = x.shape
    C = CHUNK
    NC = S // C
    proj = x @ w_in
    o1, o2, o3 = RET_QK_WIDTH, 2 * RET_QK_WIDTH, 2 * RET_QK_WIDTH + RET_V_WIDTH
    q = proj[..., :o1].reshape(B, S, RET_HEADS, RET_QK_DIM).astype(jnp.float32)
    k = proj[..., o1:o2].reshape(B, S, RET_HEADS, RET_QK_DIM).astype(jnp.float32) * (RET_QK_DIM ** -0.5)
    v = proj[..., o2:o3].reshape(B, S, RET_HEADS, RET_V_DIM).astype(jnp.float32)
    gate = proj[..., o3:].astype(jnp.float32)
    pos = jnp.arange(S, dtype=jnp.float32)
    q = rope(q, pos)
    k = rope(k, pos)

    log_gamma = jnp.log(1.0 - jnp.exp2(-5.0 - jnp.arange(RET_HEADS, dtype=jnp.float32)))
    t = jnp.arange(C, dtype=jnp.float32)
    intra_decay = jnp.exp(log_gamma[:, None, None] * jnp.abs(t[:, None] - t[None, :]))
    q_decay = jnp.exp(log_gamma[:, None] * (t[None, :] + 1.0))
    k_decay = jnp.exp(log_gamma[:, None] * (C - 1.0 - t[None, :]))
    chunk_decay = jnp.exp(log_gamma * C)

    def to_chunks(a):
        return a.reshape(B, NC, C, RET_HEADS, a.shape[-1]).transpose(1, 0, 3, 2, 4)

    def step(state, inp):
        qc, kc, vc = inp
        scores = jnp.einsum('bhtd,bhsd->bhts', qc, kc) * intra_decay
        out = jnp.einsum('bhts,bhse->bhte', scores, vc)
        out = out + jnp.einsum('bhtd,bhde->bhte', qc * q_decay[None, :, :, None], state)
        state = state * chunk_decay[None, :, None, None] + jnp.einsum(
            'bhsd,bhse->bhde', kc * k_decay[None, :, :, None], vc)
        return state, out

    state0 = jnp.zeros((B, RET_HEADS, RET_QK_DIM, RET_V_DIM), jnp.float32)
    _, o = lax.scan(step, state0, (to_chunks(q), to_chunks(k), to_chunks(v)))
    o = o.transpose(1, 0, 3, 2, 4).reshape(B, S, RET_HEADS, RET_V_DIM)
    mu = jnp.mean(o, axis=-1, keepdims=True)
    var = jnp.mean(jnp.square(o - mu), axis=-1, keepdims=True)
    o = ((o - mu) * lax.rsqrt(var + LN_EPS)).reshape(B, S, RET_V_WIDTH) * gn_g
    return (jax.nn.silu(gate) * o).astype(x.dtype) @ w_o


def peer_ffn(x, wq, sub_k1, sub_k2, u, v):
    B, S, D = x.shape
    T = B * S
    K = PEER_TOPK
    xt = x.reshape(T, D)
    q = (xt @ wq).reshape(T, PEER_HEADS, PEER_QUERY_DIM)
    s1 = jnp.einsum('thd,nd->thn', q[..., :PEER_HALF], sub_k1).astype(jnp.float32)
    s2 = jnp.einsum('thd,nd->thn', q[..., PEER_HALF:], sub_k2).astype(jnp.float32)
    v1, i1 = lax.top_k(s1, K)
    v2, i2 = lax.top_k(s2, K)
    cand = (v1[..., :, None] + v2[..., None, :]).reshape(T, PEER_HEADS, K * K)
    sc, ci = lax.top_k(cand, K)
    e1 = jnp.take_along_axis(i1, ci // K, axis=-1)
    e2 = jnp.take_along_axis(i2, ci % K, axis=-1)
    eid = e1 * PEER_N_KEYS + e2
    g = jax.nn.softmax(sc, axis=-1)

    nb = T // PEER_TOKEN_BLOCK

    def block(args):
        xb, eb, gb = args
        h = jnp.einsum('td,thkd->thk', xb, u[eb]).astype(jnp.float32)
        w = (gb * jax.nn.gelu(h, approximate=False)).astype(xb.dtype)
        return jnp.einsum('thk,thkd->td', w, v[eb])

    out = lax.map(block, (xt.reshape(nb, PEER_TOKEN_BLOCK, D),
                          eid.reshape(nb, PEER_TOKEN_BLOCK, PEER_HEADS, K),
                          g.reshape(nb, PEER_TOKEN_BLOCK, PEER_HEADS, K)))
    return out.reshape(B, S, D)


def _normal(key, shape, std):
    return jax.random.normal(key, shape, jnp.float32) * std


def _ln_params(key):
    k1, k2 = jax.random.split(key)
    return 1.0 + _normal(k1, (D_MODEL,), 0.02), _normal(k2, (D_MODEL,), 0.02)


def _peer_params(key):
    ks = jax.random.split(key, 5)
    wq = _normal(ks[0], (D_MODEL, PEER_HEADS * PEER_QUERY_DIM), D_MODEL ** -0.5)
    k1 = _normal(ks[1], (PEER_N_KEYS, PEER_HALF), PEER_HALF ** -0.5)
    k2 = _normal(ks[2], (PEER_N_KEYS, PEER_HALF), PEER_HALF ** -0.5)
    u = _normal(ks[3], (PEER_N_EXPERTS, D_MODEL), D_MODEL ** -0.5)
    v = _normal(ks[4], (PEER_N_EXPERTS, D_MODEL), DN_BETA * PEER_HEADS ** -0.5)
    return wq, k1, k2, u, v


def setup_inputs(seed: int = 0) -> dict:
    key = jax.random.key(seed)
    ks = jax.random.split(key, 16)
    D = D_MODEL
    x = jax.random.normal(ks[0], (BATCH, SEQ, D), jnp.float32)

    fox_cols = 3 * D + FOX_HEADS
    fox_scale = jnp.concatenate([jnp.ones((2 * D,), jnp.float32),
                                 jnp.full((D,), DN_BETA, jnp.float32),
                                 jnp.ones((FOX_HEADS,), jnp.float32)])
    l0_fox_w_in = _normal(ks[1], (D, fox_cols), D ** -0.5) * fox_scale
    l0_fox_b_f = jnp.linspace(3.0, 6.0, FOX_HEADS, dtype=jnp.float32) + _normal(ks[2], (FOX_HEADS,), 0.1)
    l0_fox_w_o = _normal(ks[3], (D, D), DN_BETA * D ** -0.5)
    l0_ln1_g, l0_ln1_b = _ln_params(ks[4])
    l0_peer_wq, l0_peer_k1, l0_peer_k2, l0_peer_u, l0_peer_v = _peer_params(ks[5])
    l0_ln2_g, l0_ln2_b = _ln_params(ks[6])

    ret_cols = 2 * RET_QK_WIDTH + 2 * RET_V_WIDTH
    ret_scale = jnp.concatenate([jnp.ones((2 * RET_QK_WIDTH,), jnp.float32),
                                 jnp.full((RET_V_WIDTH,), DN_BETA, jnp.float32),
                                 jnp.ones((RET_V_WIDTH,), jnp.float32)])
    l1_ret_w_in = _normal(ks[7], (D, ret_cols), D ** -0.5) * ret_scale
    l1_ret_gn_g = 1.0 + _normal(ks[8], (RET_V_WIDTH,), 0.02)
    l1_ret_w_o = _normal(ks[9], (RET_V_WIDTH, D), DN_BETA * RET_V_WIDTH ** -0.5)
    l1_ln1_g, l1_ln1_b = _ln_params(ks[10])
    l1_peer_wq, l1_peer_k1, l1_peer_k2, l1_peer_u, l1_peer_v = _peer_params(ks[11])
    l1_ln2_g, l1_ln2_b = _ln_params(ks[12])

    return {
        "x": x,
        "l0_fox_w_in": l0_fox_w_in, "l0_fox_b_f": l0_fox_b_f, "l0_fox_w_o": l0_fox_w_o,
        "l0_ln1_g": l0_ln1_g, "l0_ln1_b": l0_ln1_b,
        "l0_peer_wq": l0_peer_wq, "l0_peer_k1": l0_peer_k1, "l0_peer_k2": l0_peer_k2,
        "l0_peer_u": l0_peer_u, "l0_peer_v": l0_peer_v,
        "l0_ln2_g": l0_ln2_g, "l0_ln2_b": l0_ln2_b,
        "l1_ret_w_in": l1_ret_w_in, "l1_ret_gn_g": l1_ret_gn_g, "l1_ret_w_o": l1_ret_w_o,
        "l1_ln1_g": l1_ln1_g, "l1_ln1_b": l1_ln1_b,
        "l1_peer_wq": l1_peer_wq, "l1_peer_k1": l1_peer_k1, "l1_peer_k2": l1_peer_k2,
        "l1_peer_u": l1_peer_u, "l1_peer_v": l1_peer_v,
        "l1_ln2_g": l1_ln2_g, "l1_ln2_b": l1_ln2_b,
    }


def reference(x,
              l0_fox_w_in, l0_fox_b_f, l0_fox_w_o, l0_ln1_g, l0_ln1_b,
              l0_peer_wq, l0_peer_k1, l0_peer_k2, l0_peer_u, l0_peer_v, l0_ln2_g, l0_ln2_b,
              l1_ret_w_in, l1_ret_gn_g, l1_ret_w_o, l1_ln1_g, l1_ln1_b,
              l1_peer_wq, l1_peer_k1, l1_peer_k2, l1_peer_u, l1_peer_v, l1_ln2_g, l1_ln2_b):
    layers = [
        ((l0_fox_w_in, l0_fox_b_f, l0_fox_w_o), (l0_ln1_g, l0_ln1_b),
         (l0_peer_wq, l0_peer_k1, l0_peer_k2, l0_peer_u, l0_peer_v), (l0_ln2_g, l0_ln2_b)),
        ((l1_ret_w_in, l1_ret_gn_g, l1_ret_w_o), (l1_ln1_g, l1_ln1_b),
         (l1_peer_wq, l1_peer_k1, l1_peer_k2, l1_peer_u, l1_peer_v), (l1_ln2_g, l1_ln2_b)),
    ]
    for i in range(DEPTH):
        mixer_p, ln1_p, peer_p, ln2_p = layers[i]
        if i % N_MIXERS == 0:
            mixed = fox_mixer(x, *mixer_p)
        else:
            mixed = retention_mixer(x, *mixer_p)
        x = layer_norm(DN_ALPHA * x + mixed, *ln1_p)
        x = layer_norm(DN_ALPHA * x + peer_ffn(x, *peer_p), *ln2_p)
    return x
```

```python
import functools
import math

import jax
import jax.numpy as jnp
from jax import lax
from jax.experimental import pallas as pl
from jax.experimental.pallas import tpu as pltpu

DEPTH = 2
CHUNK = 64
FOX_HEAD_DIM = 128
RET_QK_DIM = 256
RET_V_DIM = 512
ROPE_BASE = 10000.0
PEER_N_KEYS = 128
PEER_QUERY_DIM = 256
PEER_HALF = PEER_QUERY_DIM // 2
PEER_TOPK = 16
DN_ALPHA = (2 * DEPTH) ** 0.25
LN_EPS = 1e-5

V7X_LANES = 128
V7X_SUBLANES = 8
V7X_VMEM_BYTES = 64 * 1024 * 1024
VMEM_LIMIT = V7X_VMEM_BYTES - 6 * 1024 * 1024

F32 = jnp.float32
BF16 = jnp.bfloat16
NEG_INF = float("-inf")


def _params(sem, vmem=None):
    return pltpu.CompilerParams(dimension_semantics=sem, vmem_limit_bytes=vmem or VMEM_LIMIT)


def _mm_kernel(a_ref, b_ref, o_ref, *, scale):
    acc = jnp.dot(a_ref[...], b_ref[...], preferred_element_type=F32)
    if scale is not None:
        acc = acc * scale
    o_ref[...] = acc.astype(o_ref.dtype)


def matmul(a, b, out_dtype, *, scale=None, tm=1024, tn=1024, name="mm"):
    m, k = a.shape
    _, n = b.shape
    tm = min(tm, m)
    tn = min(tn, n)
    return pl.pallas_call(
        functools.partial(_mm_kernel, scale=scale),
        grid=(n // tn, m // tm),
        in_specs=[pl.BlockSpec((tm, k), lambda j, i: (i, 0)),
                  pl.BlockSpec((k, tn), lambda j, i: (0, j))],
        out_specs=pl.BlockSpec((tm, tn), lambda j, i: (i, j)),
        out_shape=jax.ShapeDtypeStruct((m, n), out_dtype),
        compiler_params=_params(("parallel", "parallel")),
        name=name,
    )(a, b)


def _layer_norm_rows(y, g, b):
    mu = jnp.mean(y, axis=-1, keepdims=True)
    yc = y - mu
    var = jnp.mean(yc * yc, axis=-1, keepdims=True)
    return yc * lax.rsqrt(var + LN_EPS) * g + b


def _proj_ln_kernel(a_ref, w_ref, x_ref, g_ref, b_ref, of_ref, ob_ref):
    y = jnp.dot(a_ref[...], w_ref[...], preferred_element_type=F32) + DN_ALPHA * x_ref[...]
    yn = _layer_norm_rows(y, g_ref[...], b_ref[...])
    of_ref[...] = yn
    ob_ref[...] = yn.astype(BF16)


def proj_residual_ln(a, w, x, g, b, *, tm=256, name="proj_ln"):
    m, k = a.shape
    d = w.shape[1]
    tm = min(tm, m)
    return pl.pallas_call(
        _proj_ln_kernel,
        grid=(m // tm,),
        in_specs=[pl.BlockSpec((tm, k), lambda i: (i, 0)),
                  pl.BlockSpec((k, d), lambda i: (0, 0)),
                  pl.BlockSpec((tm, d), lambda i: (i, 0)),
                  pl.BlockSpec((1, d), lambda i: (0, 0)),
                  pl.BlockSpec((1, d), lambda i: (0, 0))],
        out_specs=[pl.BlockSpec((tm, d), lambda i: (i, 0)),
                   pl.BlockSpec((tm, d), lambda i: (i, 0))],
        out_shape=[jax.ShapeDtypeStruct((m, d), F32), jax.ShapeDtypeStruct((m, d), BF16)],
        compiler_params=_params(("parallel",)),
        name=name,
    )(a, w, x, g.reshape(1, d), b.reshape(1, d))


def _residual_ln_kernel(y_ref, x_ref, g_ref, b_ref, of_ref, ob_ref):
    y = y_ref[...] + DN_ALPHA * x_ref[...]
    yn = _layer_norm_rows(y, g_ref[...], b_ref[...])
    of_ref[...] = yn
    ob_ref[...] = yn.astype(BF16)


def residual_ln(y, x, g, b, *, tm=512, name="res_ln"):
    m, d = x.shape
    tm = min(tm, m)
    return pl.pallas_call(
        _residual_ln_kernel,
        grid=(m // tm,),
        in_specs=[pl.BlockSpec((tm, d), lambda i: (i, 0)),
                  pl.BlockSpec((tm, d), lambda i: (i, 0)),
                  pl.BlockSpec((1, d), lambda i: (0, 0)),
                  pl.BlockSpec((1, d), lambda i: (0, 0))],
        out_specs=[pl.BlockSpec((tm, d), lambda i: (i, 0)),
                   pl.BlockSpec((tm, d), lambda i: (i, 0))],
        out_shape=[jax.ShapeDtypeStruct((m, d), F32), jax.ShapeDtypeStruct((m, d), BF16)],
        compiler_params=_params(("parallel",)),
        name=name,
    )(y, x, g.reshape(1, d), b.reshape(1, d))


def _forget_cumsum_kernel(g_ref, bf_ref, c_ref, carry_ref):
    @pl.when(pl.program_id(1) == 0)
    def _():
        carry_ref[...] = jnp.zeros_like(carry_ref)

    z = g_ref[0] + bf_ref[...]
    lf = jnp.minimum(z, 0.0) - jnp.log1p(jnp.exp(-jnp.abs(z)))
    tc = lf.shape[0]
    row = lax.broadcasted_iota(jnp.int32, (tc, tc), 0)
    col = lax.broadcasted_iota(jnp.int32, (tc, tc), 1)
    tri = jnp.where(row >= col, 1.0, 0.0).astype(BF16)
    hi = lf.astype(BF16)
    lo = (lf - hi.astype(F32)).astype(BF16)
    cs = (jnp.dot(tri, hi, preferred_element_type=F32)
          + jnp.dot(tri, lo, preferred_element_type=F32))
    c = cs + carry_ref[...]
    c_ref[0] = c
    carry_ref[...] = c[tc - 1:tc, :]


def forget_cumsum(gate_logits, b_f_row, *, tc=512):
    bsz, s, w = gate_logits.shape
    tc = min(tc, s)
    return pl.pallas_call(
        _forget_cumsum_kernel,
        grid=(bsz, s // tc),
        in_specs=[pl.BlockSpec((1, tc, w), lambda b, i: (b, i, 0)),
                  pl.BlockSpec((1, w), lambda b, i: (0, 0))],
        out_specs=pl.BlockSpec((1, tc, w), lambda b, i: (b, i, 0)),
        out_shape=jax.ShapeDtypeStruct((bsz, s, w), F32),
        scratch_shapes=[pltpu.VMEM((1, w), F32)],
        compiler_params=_params(("parallel", "arbitrary")),
        name="forget_cumsum",
    )(gate_logits, b_f_row)


def _fox_kernel(q_ref, k_ref, v_ref, cq_ref, ck_ref, o_ref, m_ref, l_ref, acc_ref, *, tq, tk):
    qi = pl.program_id(2)
    ki = pl.program_id(3)
    nk = pl.num_programs(3)
    q_start = qi * tq
    k_start = ki * tk

    @pl.when(ki == 0)
    def _():
        m_ref[...] = jnp.full_like(m_ref, NEG_INF)
        l_ref[...] = jnp.zeros_like(l_ref)
        acc_ref[...] = jnp.zeros_like(acc_ref)

    def step(masked):
        s = lax.dot_general(q_ref[0], k_ref[0], (((1,), (1,)), ((), ())),
                            preferred_element_type=F32)
        s = s + cq_ref[0, 0] - ck_ref[0, 0]
        if masked:
            row = q_start + lax.broadcasted_iota(jnp.int32, (tq, tk), 0)
            col = k_start + lax.broadcasted_iota(jnp.int32, (tq, tk), 1)
            s = jnp.where(row >= col, s, NEG_INF)
        m_prev = m_ref[...]
        m_new = jnp.maximum(m_prev, jnp.max(s, axis=-1, keepdims=True))
        alpha = jnp.exp(m_prev - m_new)
        p = jnp.exp(s - m_new)
        l_ref[...] = alpha * l_ref[...] + jnp.sum(p, axis=-1, keepdims=True)
        acc_ref[...] = alpha * acc_ref[...] + jnp.dot(p.astype(BF16), v_ref[0],
                                                      preferred_element_type=F32)
        m_ref[...] = m_new

    k_end = k_start + tk - 1

    @pl.when(k_end <= q_start)
    def _():
        step(False)

    @pl.when(jnp.logical_and(k_end > q_start, k_start <= q_start + tq - 1))
    def _():
        step(True)

    @pl.when(ki == nk - 1)
    def _():
        o_ref[0] = (acc_ref[...] / l_ref[...]).astype(o_ref.dtype)


def fox_attention(q, kv, cq, ck, *, tq=512, tk=512):
    bsz, s, d = q.shape
    nh = d // FOX_HEAD_DIM
    tq = min(tq, s)
    tk = min(tk, s)
    nq, nk = s // tq, s // tk

    def last_k(qi):
        return (qi * tq + tq - 1) // tk

    return pl.pallas_call(
        functools.partial(_fox_kernel, tq=tq, tk=tk),
        grid=(bsz, nh, nq, nk),
        in_specs=[
            pl.BlockSpec((1, tq, FOX_HEAD_DIM), lambda b, h, qi, ki: (b, qi, h)),
            pl.BlockSpec((1, tk, FOX_HEAD_DIM), lambda b, h, qi, ki: (b, jnp.minimum(ki, last_k(qi)), h)),
            pl.BlockSpec((1, tk, FOX_HEAD_DIM), lambda b, h, qi, ki: (b, jnp.minimum(ki, last_k(qi)), nh + h)),
            pl.BlockSpec((1, 1, tq, 1), lambda b, h, qi, ki: (b, h, qi, 0)),
            pl.BlockSpec((1, 1, 1, tk), lambda b, h, qi, ki: (b, h, 0, jnp.minimum(ki, last_k(qi)))),
        ],
        out_specs=pl.BlockSpec((1, tq, FOX_HEAD_DIM), lambda b, h, qi, ki: (b, qi, h)),
        out_shape=jax.ShapeDtypeStruct((bsz, s, d), BF16),
        scratch_shapes=[pltpu.VMEM((tq, 1), F32), pltpu.VMEM((tq, 1), F32),
                        pltpu.VMEM((tq, FOX_HEAD_DIM), F32)],
        compiler_params=_params(("parallel", "parallel", "parallel", "arbitrary")),
        name="fox_attention",
    )(q, kv, kv, cq, ck)


def _rope_rows(a, cos, sin):
    half = a.shape[-1] // 2
    a1, a2 = a[:, :half], a[:, half:]
    return jnp.concatenate([a1 * cos - a2 * sin, a1 * sin + a2 * cos], axis=-1)


def _retention_kernel(q_ref, k_ref, v_ref, gate_ref, cos_ref, sin_ref, dmat_ref, qd_ref, kd_ref,
                      bd_ref, gn_ref, o_ref, state_ref):
    @pl.when(pl.program_id(2) == 0)
    def _():
        state_ref[...] = jnp.zeros_like(state_ref)

    cos = cos_ref[...]
    sin = sin_ref[...]
    q = _rope_rows(q_ref[0], cos, sin)
    k = _rope_rows(k_ref[0], cos, sin) * (RET_QK_DIM ** -0.5)
    v = v_ref[0]
    scores = lax.dot_general(q.astype(BF16), k.astype(BF16), (((1,), (1,)), ((), ())),
                             preferred_element_type=F32) * dmat_ref[0]
    out = jnp.dot(scores.astype(BF16), v, preferred_element_type=F32)
    state = state_ref[...]
    out = out + jnp.dot((q * qd_ref[0]).astype(BF16), state.astype(BF16),
                        preferred_element_type=F32)
    kdt = jnp.transpose(k * kd_ref[0]).astype(BF16)
    state_ref[...] = state * bd_ref[0] + jnp.dot(kdt, v, preferred_element_type=F32)
    mu = jnp.mean(out, axis=-1, keepdims=True)
    oc = out - mu
    var = jnp.mean(oc * oc, axis=-1, keepdims=True)
    o = oc * lax.rsqrt(var + LN_EPS) * gn_ref[...]
    g = gate_ref[0].astype(F32)
    o_ref[0] = (g * jax.nn.sigmoid(g) * o).astype(o_ref.dtype)


def _retention_tables(nh, blk):
    log_gamma = jnp.log(1.0 - jnp.exp2(-5.0 - jnp.arange(nh, dtype=F32)))
    t = jnp.arange(blk, dtype=F32)
    chunk_id = jnp.arange(blk) // CHUNK
    dist = jnp.abs(t[:, None] - t[None, :])
    allowed = chunk_id[None, :] <= chunk_id[:, None]
    dmat = jnp.where(allowed[None], jnp.exp(log_gamma[:, None, None] * dist[None]), 0.0)
    qd = jnp.exp(log_gamma[:, None] * (t[None, :] + 1.0))[..., None]
    kd = jnp.exp(log_gamma[:, None] * (blk - 1.0 - t[None, :]))[..., None]
    bd = jnp.exp(log_gamma * blk).reshape(nh, 1, 1)
    return dmat.astype(F32), qd, kd, bd


def retention(qk, vg, cos, sin, gn_g, *, blk=256):
    bsz, s, w = qk.shape
    nh = w // (2 * RET_QK_DIM)
    blk = min(blk, s)
    dmat, qd, kd, bd = _retention_tables(nh, blk)
    half = RET_QK_DIM // 2
    return pl.pallas_call(
        _retention_kernel,
        grid=(bsz, nh, s // blk),
        in_specs=[
            pl.BlockSpec((1, blk, RET_QK_DIM), lambda b, h, i: (b, i, h)),
            pl.BlockSpec((1, blk, RET_QK_DIM), lambda b, h, i: (b, i, nh + h)),
            pl.BlockSpec((1, blk, RET_V_DIM), lambda b, h, i: (b, i, h)),
            pl.BlockSpec((1, blk, RET_V_DIM), lambda b, h, i: (b, i, nh + h)),
            pl.BlockSpec((blk, half), lambda b, h, i: (i, 0)),
            pl.BlockSpec((blk, half), lambda b, h, i: (i, 0)),
            pl.BlockSpec((1, blk, blk), lambda b, h, i: (h, 0, 0)),
            pl.BlockSpec((1, blk, 1), lambda b, h, i: (h, 0, 0)),
            pl.BlockSpec((1, blk, 1), lambda b, h, i: (h, 0, 0)),
            pl.BlockSpec((1, 1, 1), lambda b, h, i: (h, 0, 0)),
            pl.BlockSpec((1, RET_V_DIM), lambda b, h, i: (0, h)),
        ],
        out_specs=pl.BlockSpec((1, blk, RET_V_DIM), lambda b, h, i: (b, i, h)),
        out_shape=jax.ShapeDtypeStruct((bsz, s, nh * RET_V_DIM), BF16),
        scratch_shapes=[pltpu.VMEM((RET_QK_DIM, RET_V_DIM), F32)],
        compiler_params=_params(("parallel", "parallel", "arbitrary")),
        name="retention",
    )(qk, qk, vg, vg, cos, sin, dmat, qd, kd, bd, gn_g.reshape(1, -1))


def _peer_scores_kernel(x_ref, wq_ref, k1_ref, k2_ref, s1_ref, s2_ref):
    q = jnp.dot(x_ref[...], wq_ref[...], preferred_element_type=F32).astype(BF16)
    nh = s1_ref.shape[0]
    nt = (((1,), (1,)), ((), ()))
    for h in range(nh):
        lo = h * PEER_QUERY_DIM
        s1_ref[h] = lax.dot_general(k1_ref[...], q[:, lo:lo + PEER_HALF], nt,
                                    preferred_element_type=F32)
        s2_ref[h] = lax.dot_general(k2_ref[...], q[:, lo + PEER_HALF:lo + PEER_QUERY_DIM], nt,
                                    preferred_element_type=F32)


def peer_scores(xb, wq, k1, k2, *, tm=256):
    t, d = xb.shape
    nh = wq.shape[1] // PEER_QUERY_DIM
    tm = min(tm, t)
    shp = jax.ShapeDtypeStruct((nh, PEER_N_KEYS, t), F32)
    return pl.pallas_call(
        _peer_scores_kernel,
        grid=(t // tm,),
        in_specs=[pl.BlockSpec((tm, d), lambda i: (i, 0)),
                  pl.BlockSpec(wq.shape, lambda i: (0, 0)),
                  pl.BlockSpec(k1.shape, lambda i: (0, 0)),
                  pl.BlockSpec(k2.shape, lambda i: (0, 0))],
        out_specs=[pl.BlockSpec((nh, PEER_N_KEYS, tm), lambda i: (0, 0, i)),
                   pl.BlockSpec((nh, PEER_N_KEYS, tm), lambda i: (0, 0, i))],
        out_shape=[shp, shp],
        compiler_params=_params(("parallel",)),
        name="peer_scores",
    )(xb, wq, k1, k2)


_PEER_CANDS = tuple((i, j) for i in range(PEER_TOPK) for j in range(PEER_TOPK)
                    if (i + 1) * (j + 1) <= PEER_TOPK)


def _top16_keys(s):
    key = lax.broadcasted_iota(jnp.int32, s.shape, 1)
    vals, idxs = [], []
    for _ in range(PEER_TOPK):
        m = jnp.max(s, axis=1, keepdims=True)
        idx = jnp.min(jnp.where(s == m, key, PEER_N_KEYS), axis=1, keepdims=True)
        s = jnp.where(key == idx, NEG_INF, s)
        vals.append(m[:, 0, :])
        idxs.append(idx[:, 0, :])
    return vals, idxs


def _peer_topk_kernel(s1_ref, s2_ref, eid_ref, g_ref):
    v1, i1 = _top16_keys(s1_ref[...])
    v2, i2 = _top16_keys(s2_ref[...])
    cand = [v1[i] + v2[j] for (i, j) in _PEER_CANDS]
    ceid = [i1[i] * PEER_N_KEYS + i2[j] for (i, j) in _PEER_CANDS]
    ncand = len(_PEER_CANDS)
    scores, eids = [], []
    for _ in range(PEER_TOPK):
        m = functools.reduce(jnp.maximum, cand)
        pos = functools.reduce(jnp.minimum,
                               [jnp.where(cand[c] == m, c, ncand) for c in range(ncand)])
        e = jnp.zeros_like(ceid[0])
        for c in range(ncand):
            hit = pos == c
            e = jnp.where(hit, ceid[c], e)
            cand[c] = jnp.where(hit, NEG_INF, cand[c])
        scores.append(m)
        eids.append(e)
    ex = [jnp.exp(sc - scores[0]) for sc in scores]
    denom = functools.reduce(jnp.add, ex)
    for k in range(PEER_TOPK):
        eid_ref[k] = eids[k]
        g_ref[k] = ex[k] / denom


def peer_topk(s1, s2, *, tt=128):
    nh, nk, t = s1.shape
    tt = min(tt, t)
    return pl.pallas_call(
        _peer_topk_kernel,
        grid=(t // tt,),
        in_specs=[pl.BlockSpec((nh, nk, tt), lambda i: (0, 0, i)),
                  pl.BlockSpec((nh, nk, tt), lambda i: (0, 0, i))],
        out_specs=[pl.BlockSpec((PEER_TOPK, nh, tt), lambda i: (0, 0, i)),
                   pl.BlockSpec((PEER_TOPK, nh, tt), lambda i: (0, 0, i))],
        out_shape=[jax.ShapeDtypeStruct((PEER_TOPK, nh, t), jnp.int32),
                   jax.ShapeDtypeStruct((PEER_TOPK, nh, t), F32)],
        compiler_params=_params(("parallel",)),
        name="peer_topk",
    )(s1, s2)


def _sublane_group_sums(ps):
    sub = lax.broadcasted_iota(jnp.int32, ps[0].shape, 0)
    vals = list(ps)
    for gsz in (1, 2, 4):
        low = (sub // gsz) % 2 == 0
        nxt = []
        for a, b in zip(vals[0::2], vals[1::2]):
            keep = jnp.where(low, a, b)
            move = jnp.where(low, b, a)
            nxt.append(keep + pltpu.roll(move, shift=gsz, axis=0))
        vals = nxt
    return vals[0]


def _peer_up_kernel(eid_ref, x_ref, tab_ref, o_ref, *, npairs):
    tb = x_ref.shape[0]
    ngroups = npairs // V7X_SUBLANES
    lane = lax.broadcasted_iota(jnp.int32, (V7X_SUBLANES, tb), 1)

    def token(t, accs):
        xt = x_ref[t]
        new = []
        for g in range(ngroups):
            prods = []
            for j in range(V7X_SUBLANES):
                e = eid_ref[t, g * V7X_SUBLANES + j]
                prods.append(tab_ref[e].astype(F32) * xt)
            r = jnp.sum(_sublane_group_sums(prods), axis=-1, keepdims=True)
            new.append(jnp.where(lane == t, r, accs[g]))
        return tuple(new)

    init = tuple(jnp.zeros((V7X_SUBLANES, tb), F32) for _ in range(ngroups))
    accs = lax.fori_loop(0, tb, token, init)
    for g in range(ngroups):
        o_ref[g * V7X_SUBLANES:(g + 1) * V7X_SUBLANES, :] = accs[g]


def peer_up(eid_tok, x4, utab, *, tb=128):
    t, npairs = eid_tok.shape
    nhalf, nexp, r, lanes = utab.shape
    assert tb == V7X_LANES and t % tb == 0
    return pl.pallas_call(
        functools.partial(_peer_up_kernel, npairs=npairs),
        grid=(nhalf, t // tb),
        in_specs=[pl.BlockSpec((tb, npairs), lambda c, i: (i, 0), memory_space=pltpu.SMEM),
                  pl.BlockSpec((tb, None, r, lanes), lambda c, i: (i, c, 0, 0)),
                  pl.BlockSpec((None, nexp, r, lanes), lambda c, i: (c, 0, 0, 0),
                               pipeline_mode=pl.Buffered(1))],
        out_specs=pl.BlockSpec((None, npairs, tb), lambda c, i: (c, 0, i)),
        out_shape=jax.ShapeDtypeStruct((nhalf, npairs, t), F32),
        compiler_params=_params(("arbitrary", "arbitrary")),
        name="peer_up",
    )(eid_tok, x4, utab)


def _peer_act_kernel(h_ref, g_ref, w_ref):
    h = h_ref[0] + h_ref[1]
    gelu = 0.5 * h * (1.0 + lax.erf(h * (2.0 ** -0.5)))
    w_ref[...] = g_ref[...] * gelu


def peer_act(h2, g, *, tt=512):
    _, npairs, t = h2.shape
    tt = min(tt, t)
    return pl.pallas_call(
        _peer_act_kernel,
        grid=(t // tt,),
        in_specs=[pl.BlockSpec((2, npairs, tt), lambda i: (0, 0, i)),
                  pl.BlockSpec((npairs, tt), lambda i: (0, i))],
        out_specs=pl.BlockSpec((npairs, tt), lambda i: (0, i)),
        out_shape=jax.ShapeDtypeStruct((npairs, t), F32),
        compiler_params=_params(("parallel",)),
        name="peer_act",
    )(h2, g)


def _peer_down_kernel(eid_ref, w_ref, tab_ref, o_ref, *, npairs):
    tb = o_ref.shape[0]
    nacc = 4

    def token(t, carry):
        accs = [jnp.zeros(o_ref.shape[1:], F32) for _ in range(nacc)]
        for p in range(npairs):
            e = eid_ref[t, p]
            w = w_ref[t, p]
            accs[p % nacc] = accs[p % nacc] + tab_ref[e].astype(F32) * w
        o_ref[t] = (accs[0] + accs[1]) + (accs[2] + accs[3])
        return carry

    lax.fori_loop(0, tb, token, 0)


def peer_down(eid_tok, w_tok, vtab, *, tb=128):
    t, npairs = eid_tok.shape
    nhalf, nexp, r, lanes = vtab.shape
    tb = min(tb, t)
    return pl.pallas_call(
        functools.partial(_peer_down_kernel, npairs=npairs),
        grid=(nhalf, t // tb),
        in_specs=[pl.BlockSpec((tb, npairs), lambda c, i: (i, 0), memory_space=pltpu.SMEM),
                  pl.BlockSpec((tb, npairs), lambda c, i: (i, 0), memory_space=pltpu.SMEM),
                  pl.BlockSpec((None, nexp, r, lanes), lambda c, i: (c, 0, 0, 0),
                               pipeline_mode=pl.Buffered(1))],
        out_specs=pl.BlockSpec((tb, None, r, lanes), lambda c, i: (i, c, 0, 0)),
        out_shape=jax.ShapeDtypeStruct((t, nhalf, r, lanes), F32),
        compiler_params=_params(("arbitrary", "arbitrary")),
        name="peer_down",
    )(eid_tok, w_tok, vtab)


def _expert_table(w):
    nexp, d = w.shape
    r = d // (2 * V7X_LANES)
    return jnp.transpose(w.astype(BF16).reshape(nexp, 2, r, V7X_LANES), (1, 0, 2, 3))


def peer_ffn(x_f32, x_bf16, wq, k1, k2, u, v):
    t, d = x_f32.shape
    r = d // (2 * V7X_LANES)
    s1, s2 = peer_scores(x_bf16, wq.astype(BF16), k1.astype(BF16), k2.astype(BF16))
    eid, gate = peer_topk(s1, s2)
    npairs = eid.shape[0] * eid.shape[1]
    eid_tok = eid.reshape(npairs, t).T
    h2 = peer_up(eid_tok, x_f32.reshape(t, 2, r, V7X_LANES), _expert_table(u))
    w = peer_act(h2, gate.reshape(npairs, t))
    y = peer_down(eid_tok, w.T, _expert_table(v))
    return y.reshape(t, d)


def fox_layer(x_f32, x_bf16, bsz, s, w_in, b_f, w_o, ln_g, ln_b):
    t, d = x_f32.shape
    nh = d // FOX_HEAD_DIM
    wb = w_in.astype(BF16)
    q = matmul(x_bf16, wb[:, :d], BF16, scale=FOX_HEAD_DIM ** -0.5, name="fox_q")
    kv = matmul(x_bf16, wb[:, d:3 * d], BF16, name="fox_kv")
    wg = jnp.pad(wb[:, 3 * d:], ((0, 0), (0, V7X_LANES - nh)))
    gl = matmul(x_bf16, wg, F32, tn=V7X_LANES, name="fox_gate")
    bf_row = jnp.pad(b_f.astype(F32), (0, V7X_LANES - nh)).reshape(1, V7X_LANES)
    c = forget_cumsum(gl.reshape(bsz, s, V7X_LANES), bf_row)[:, :, :nh]
    ct = jnp.transpose(c, (0, 2, 1))
    o = fox_attention(q.reshape(bsz, s, d), kv.reshape(bsz, s, 2 * d),
                      ct[..., None], ct[:, :, None, :])
    return proj_residual_ln(o.reshape(t, d), w_o.astype(BF16), x_f32, ln_g, ln_b, name="fox_out_ln")


def retention_layer(x_f32, x_bf16, bsz, s, w_in, gn_g, w_o, ln_g, ln_b):
    t, d = x_f32.shape
    nh = d // RET_QK_DIM
    qkw = 2 * nh * RET_QK_DIM
    wb = w_in.astype(BF16)
    qk = matmul(x_bf16, wb[:, :qkw], F32, name="ret_qk")
    vg = matmul(x_bf16, wb[:, qkw:], BF16, name="ret_vg")
    half = RET_QK_DIM // 2
    inv_freq = ROPE_BASE ** (-jnp.arange(half, dtype=F32) / half)
    ang = jnp.arange(s, dtype=F32)[:, None] * inv_freq[None, :]
    o = retention(qk.reshape(bsz, s, -1), vg.reshape(bsz, s, -1), jnp.cos(ang), jnp.sin(ang), gn_g)
    return proj_residual_ln(o.reshape(t, -1), w_o.astype(BF16), x_f32, ln_g, ln_b, name="ret_out_ln")


def kernel(x, l0_fox_w_in, l0_fox_b_f, l0_fox_w_o, l0_ln1_g, l0_ln1_b, l0_peer_wq, l0_peer_k1, l0_peer_k2, l0_peer_u, l0_peer_v, l0_ln2_g, l0_ln2_b, l1_ret_w_in, l1_ret_gn_g, l1_ret_w_o, l1_ln1_g, l1_ln1_b, l1_peer_wq, l1_peer_k1, l1_peer_k2, l1_peer_u, l1_peer_v, l1_ln2_g, l1_ln2_b):
    bsz, s, d = x.shape
    t = bsz * s
    xf = x.reshape(t, d).astype(F32)
    xb = xf.astype(BF16)

    xf, xb = fox_layer(xf, xb, bsz, s, l0_fox_w_in, l0_fox_b_f, l0_fox_w_o, l0_ln1_g, l0_ln1_b)
    y = peer_ffn(xf, xb, l0_peer_wq, l0_peer_k1, l0_peer_k2, l0_peer_u, l0_peer_v)
    xf, xb = residual_ln(y, xf, l0_ln2_g, l0_ln2_b, name="peer0_ln")

    xf, xb = retention_layer(xf, xb, bsz, s, l1_ret_w_in, l1_ret_gn_g, l1_ret_w_o, l1_ln1_g, l1_ln1_b)
    y = peer_ffn(xf, xb, l1_peer_wq, l1_peer_k1, l1_peer_k2, l1_peer_u, l1_peer_v)
    xf, _ = residual_ln(y, xf, l1_ln2_g, l1_ln2_b, name="peer1_ln")
    return xf.reshape(bsz, s, d).astype(x.dtype)
```

```python
import functools
import math

import jax
import jax.numpy as jnp
from jax import lax
from jax.experimental import pallas as pl
from jax.experimental.pallas import tpu as pltpu

DEPTH = 2
CHUNK = 64
FOX_HEAD_DIM = 128
RET_QK_DIM = 256
RET_V_DIM = 512
ROPE_BASE = 10000.0
PEER_N_KEYS = 128
PEER_QUERY_DIM = 256
PEER_HALF = PEER_QUERY_DIM // 2
PEER_TOPK = 16
DN_ALPHA = (2 * DEPTH) ** 0.25
LN_EPS = 1e-5

V7X_LANES = 128
V7X_SUBLANES = 8
V7X_VMEM_BYTES = 64 * 1024 * 1024
VMEM_LIMIT = V7X_VMEM_BYTES - 6 * 1024 * 1024

F32 = jnp.float32
BF16 = jnp.bfloat16
NEG_INF = float("-inf")


def _params(sem, vmem=None):
    return pltpu.CompilerParams(dimension_semantics=sem, vmem_limit_bytes=vmem or VMEM_LIMIT)


def _mm_kernel(a_ref, b_ref, o_ref, *, scale):
    acc = jnp.dot(a_ref[...], b_ref[...], preferred_element_type=F32)
    if scale is not None:
        acc = acc * scale
    o_ref[...] = acc.astype(o_ref.dtype)


def matmul(a, b, out_dtype, *, scale=None, tm=1024, tn=1024, name="mm"):
    m, k = a.shape
    _, n = b.shape
    tm = min(tm, m)
    tn = min(tn, n)
    return pl.pallas_call(
        functools.partial(_mm_kernel, scale=scale),
        grid=(n // tn, m // tm),
        in_specs=[pl.BlockSpec((tm, k), lambda j, i: (i, 0)),
                  pl.BlockSpec((k, tn), lambda j, i: (0, j))],
        out_specs=pl.BlockSpec((tm, tn), lambda j, i: (i, j)),
        out_shape=jax.ShapeDtypeStruct((m, n), out_dtype),
        compiler_params=_params(("parallel", "parallel")),
        name=name,
    )(a, b)


def _layer_norm_rows(y, g, b):
    mu = jnp.mean(y, axis=-1, keepdims=True)
    yc = y - mu
    var = jnp.mean(yc * yc, axis=-1, keepdims=True)
    return yc * lax.rsqrt(var + LN_EPS) * g + b


def _proj_ln_kernel(a_ref, w_ref, x_ref, g_ref, b_ref, of_ref, ob_ref):
    y = jnp.dot(a_ref[...], w_ref[...], preferred_element_type=F32) + DN_ALPHA * x_ref[...]
    yn = _layer_norm_rows(y, g_ref[...], b_ref[...])
    of_ref[...] = yn
    ob_ref[...] = yn.astype(BF16)


def proj_residual_ln(a, w, x, g, b, *, tm=256, name="proj_ln"):
    m, k = a.shape
    d = w.shape[1]
    tm = min(tm, m)
    return pl.pallas_call(
        _proj_ln_kernel,
        grid=(m // tm,),
        in_specs=[pl.BlockSpec((tm, k), lambda i: (i, 0)),
                  pl.BlockSpec((k, d), lambda i: (0, 0)),
                  pl.BlockSpec((tm, d), lambda i: (i, 0)),
                  pl.BlockSpec((1, d), lambda i: (0, 0)),
                  pl.BlockSpec((1, d), lambda i: (0, 0))],
        out_specs=[pl.BlockSpec((tm, d), lambda i: (i, 0)),
                   pl.BlockSpec((tm, d), lambda i: (i, 0))],
        out_shape=[jax.ShapeDtypeStruct((m, d), F32), jax.ShapeDtypeStruct((m, d), BF16)],
        compiler_params=_params(("parallel",)),
        name=name,
    )(a, w, x, g.reshape(1, d), b.reshape(1, d))


def _residual_ln_kernel(y_ref, x_ref, g_ref, b_ref, of_ref, ob_ref):
    y = y_ref[...] + DN_ALPHA * x_ref[...]
    yn = _layer_norm_rows(y, g_ref[...], b_ref[...])
    of_ref[...] = yn
    ob_ref[...] = yn.astype(BF16)


def residual_ln(y, x, g, b, *, tm=512, name="res_ln"):
    m, d = x.shape
    tm = min(tm, m)
    return pl.pallas_call(
        _residual_ln_kernel,
        grid=(m // tm,),
        in_specs=[pl.BlockSpec((tm, d), lambda i: (i, 0)),
                  pl.BlockSpec((tm, d), lambda i: (i, 0)),
                  pl.BlockSpec((1, d), lambda i: (0, 0)),
                  pl.BlockSpec((1, d), lambda i: (0, 0))],
        out_specs=[pl.BlockSpec((tm, d), lambda i: (i, 0)),
                   pl.BlockSpec((tm, d), lambda i: (i, 0))],
        out_shape=[jax.ShapeDtypeStruct((m, d), F32), jax.ShapeDtypeStruct((m, d), BF16)],
        compiler_params=_params(("parallel",)),
        name=name,
    )(y, x, g.reshape(1, d), b.reshape(1, d))


def _forget_cumsum_kernel(g_ref, bf_ref, c_ref, carry_ref):
    @pl.when(pl.program_id(1) == 0)
    def _():
        carry_ref[...] = jnp.zeros_like(carry_ref)

    z = g_ref[0] + bf_ref[...]
    lf = jnp.minimum(z, 0.0) - jnp.log1p(jnp.exp(-jnp.abs(z)))
    tc = lf.shape[0]
    row = lax.broadcasted_iota(jnp.int32, (tc, tc), 0)
    col = lax.broadcasted_iota(jnp.int32, (tc, tc), 1)
    tri = jnp.where(row >= col, 1.0, 0.0).astype(BF16)
    hi = lf.astype(BF16)
    lo = (lf - hi.astype(F32)).astype(BF16)
    cs = (jnp.dot(tri, hi, preferred_element_type=F32)
          + jnp.dot(tri, lo, preferred_element_type=F32))
    c = cs + carry_ref[...]
    c_ref[0] = c
    carry_ref[...] = c[tc - 1:tc, :]


def forget_cumsum(gate_logits, b_f_row, *, tc=512):
    bsz, s, w = gate_logits.shape
    tc = min(tc, s)
    return pl.pallas_call(
        _forget_cumsum_kernel,
        grid=(bsz, s // tc),
        in_specs=[pl.BlockSpec((1, tc, w), lambda b, i: (b, i, 0)),
                  pl.BlockSpec((1, w), lambda b, i: (0, 0))],
        out_specs=pl.BlockSpec((1, tc, w), lambda b, i: (b, i, 0)),
        out_shape=jax.ShapeDtypeStruct((bsz, s, w), F32),
        scratch_shapes=[pltpu.VMEM((1, w), F32)],
        compiler_params=_params(("parallel", "arbitrary")),
        name="forget_cumsum",
    )(gate_logits, b_f_row)


LOG2E = math.log2(math.e)


FOX_HEADS_PER_STEP = 2


def _fox_kernel(qi_ref, ki_ref, q_ref, k_ref, v_ref, cq_ref, ck_ref, o_ref, m_ref, l_ref, acc_ref, *, tb):
    step_id = pl.program_id(2)
    qi = qi_ref[step_id]
    ki = ki_ref[step_id]
    hd = FOX_HEAD_DIM

    @pl.when(ki == 0)
    def _():
        m_ref[...] = jnp.full_like(m_ref, NEG_INF)
        l_ref[...] = jnp.zeros_like(l_ref)
        acc_ref[...] = jnp.zeros_like(acc_ref)

    def step(masked, last):
        for g in range(FOX_HEADS_PER_STEP):
            cols = slice(g * hd, (g + 1) * hd)
            s = lax.dot_general(q_ref[0, :, cols], k_ref[0, :, cols], (((1,), (1,)), ((), ())),
                                preferred_element_type=F32)
            cq = cq_ref[0, g] * LOG2E
            z = s - ck_ref[0, g] * LOG2E
            if masked:
                row = lax.broadcasted_iota(jnp.int32, (tb, tb), 0)
                col = lax.broadcasted_iota(jnp.int32, (tb, tb), 1)
                z = jnp.where(row >= col, z, NEG_INF)
            m_prev = m_ref[g]
            m_new = jnp.maximum(m_prev, cq + jnp.max(z, axis=-1, keepdims=True))
            alpha = jnp.exp2(m_prev - m_new)
            p = jnp.exp2(z + (cq - m_new))
            l_new = alpha * l_ref[g] + jnp.sum(p, axis=-1, keepdims=True)
            acc_new = alpha * acc_ref[g] + jnp.dot(p.astype(BF16), v_ref[0, :, cols],
                                                   preferred_element_type=F32)
            if last:
                o_ref[0, :, cols] = (acc_new / l_new).astype(o_ref.dtype)
            else:
                m_ref[g] = m_new
                l_ref[g] = l_new
                acc_ref[g] = acc_new

    @pl.when(ki < qi)
    def _():
        step(False, False)

    @pl.when(ki == qi)
    def _():
        step(True, True)


def fox_attention(q, kv, cq, ck, *, tb=512):
    bsz, s, d = q.shape
    nh = d // FOX_HEAD_DIM
    hps = FOX_HEADS_PER_STEP
    assert nh % hps == 0
    tb = min(tb, s)
    nb = s // tb
    width = hps * FOX_HEAD_DIM
    pairs = [(qi, ki) for qi in range(nb) for ki in range(qi + 1)]
    qi_arr = jnp.asarray([p[0] for p in pairs], jnp.int32)
    ki_arr = jnp.asarray([p[1] for p in pairs], jnp.int32)
    grid_spec = pltpu.PrefetchScalarGridSpec(
        num_scalar_prefetch=2,
        grid=(bsz, nh // hps, len(pairs)),
        in_specs=[
            pl.BlockSpec((1, tb, width), lambda b, h, i, qa, ka: (b, qa[i], h)),
            pl.BlockSpec((1, tb, width), lambda b, h, i, qa, ka: (b, ka[i], h)),
            pl.BlockSpec((1, tb, width), lambda b, h, i, qa, ka: (b, ka[i], nh // hps + h)),
            pl.BlockSpec((1, hps, tb, 1), lambda b, h, i, qa, ka: (b, h, qa[i], 0)),
            pl.BlockSpec((1, hps, 1, tb), lambda b, h, i, qa, ka: (b, h, 0, ka[i])),
        ],
        out_specs=pl.BlockSpec((1, tb, width), lambda b, h, i, qa, ka: (b, qa[i], h)),
        scratch_shapes=[pltpu.VMEM((hps, tb, 1), F32), pltpu.VMEM((hps, tb, 1), F32),
                        pltpu.VMEM((hps, tb, FOX_HEAD_DIM), F32)],
    )
    return pl.pallas_call(
        functools.partial(_fox_kernel, tb=tb),
        grid_spec=grid_spec,
        out_shape=jax.ShapeDtypeStruct((bsz, s, d), BF16),
        compiler_params=_params(("parallel", "parallel", "arbitrary")),
        name="fox_attention",
    )(qi_arr, ki_arr, q, kv, kv, cq, ck)


def _rope_rows(a, cos, sin):
    half = a.shape[-1] // 2
    a1, a2 = a[:, :half], a[:, half:]
    return jnp.concatenate([a1 * cos - a2 * sin, a1 * sin + a2 * cos], axis=-1)


def _retention_kernel(q_ref, k_ref, v_ref, gate_ref, cos_ref, sin_ref, dmat_ref, qd_ref, kd_ref,
                      bd_ref, gn_ref, o_ref, state_ref):
    @pl.when(pl.program_id(2) == 0)
    def _():
        state_ref[...] = jnp.zeros_like(state_ref)

    cos = cos_ref[...]
    sin = sin_ref[...]
    q = _rope_rows(q_ref[0], cos, sin)
    k = _rope_rows(k_ref[0], cos, sin) * (RET_QK_DIM ** -0.5)
    v = v_ref[0]
    scores = lax.dot_general(q.astype(BF16), k.astype(BF16), (((1,), (1,)), ((), ())),
                             preferred_element_type=F32) * dmat_ref[0]
    out = jnp.dot(scores.astype(BF16), v, preferred_element_type=F32)
    state = state_ref[...]
    out = out + jnp.dot((q * qd_ref[0]).astype(BF16), state.astype(BF16),
                        preferred_element_type=F32)
    kdt = jnp.transpose(k * kd_ref[0]).astype(BF16)
    state_ref[...] = state * bd_ref[0] + jnp.dot(kdt, v, preferred_element_type=F32)
    mu = jnp.mean(out, axis=-1, keepdims=True)
    oc = out - mu
    var = jnp.mean(oc * oc, axis=-1, keepdims=True)
    o = oc * lax.rsqrt(var + LN_EPS) * gn_ref[...]
    g = gate_ref[0].astype(F32)
    o_ref[0] = (g * jax.nn.sigmoid(g) * o).astype(o_ref.dtype)


def _retention_tables(nh, blk):
    log_gamma = jnp.log(1.0 - jnp.exp2(-5.0 - jnp.arange(nh, dtype=F32)))
    t = jnp.arange(blk, dtype=F32)
    chunk_id = jnp.arange(blk) // CHUNK
    dist = jnp.abs(t[:, None] - t[None, :])
    allowed = chunk_id[None, :] <= chunk_id[:, None]
    dmat = jnp.where(allowed[None], jnp.exp(log_gamma[:, None, None] * dist[None]), 0.0)
    qd = jnp.exp(log_gamma[:, None] * (t[None, :] + 1.0))[..., None]
    kd = jnp.exp(log_gamma[:, None] * (blk - 1.0 - t[None, :]))[..., None]
    bd = jnp.exp(log_gamma * blk).reshape(nh, 1, 1)
    return dmat.astype(F32), qd, kd, bd


def retention(qk, vg, cos, sin, gn_g, *, blk=256):
    bsz, s, w = qk.shape
    nh = w // (2 * RET_QK_DIM)
    blk = min(blk, s)
    dmat, qd, kd, bd = _retention_tables(nh, blk)
    half = RET_QK_DIM // 2
    return pl.pallas_call(
        _retention_kernel,
        grid=(bsz, nh, s // blk),
        in_specs=[
            pl.BlockSpec((1, blk, RET_QK_DIM), lambda b, h, i: (b, i, h)),
            pl.BlockSpec((1, blk, RET_QK_DIM), lambda b, h, i: (b, i, nh + h)),
            pl.BlockSpec((1, blk, RET_V_DIM), lambda b, h, i: (b, i, h)),
            pl.BlockSpec((1, blk, RET_V_DIM), lambda b, h, i: (b, i, nh + h)),
            pl.BlockSpec((blk, half), lambda b, h, i: (i, 0)),
            pl.BlockSpec((blk, half), lambda b, h, i: (i, 0)),
            pl.BlockSpec((1, blk, blk), lambda b, h, i: (h, 0, 0)),
            pl.BlockSpec((1, blk, 1), lambda b, h, i: (h, 0, 0)),
            pl.BlockSpec((1, blk, 1), lambda b, h, i: (h, 0, 0)),
            pl.BlockSpec((1, 1, 1), lambda b, h, i: (h, 0, 0)),
            pl.BlockSpec((1, RET_V_DIM), lambda b, h, i: (0, h)),
        ],
        out_specs=pl.BlockSpec((1, blk, RET_V_DIM), lambda b, h, i: (b, i, h)),
        out_shape=jax.ShapeDtypeStruct((bsz, s, nh * RET_V_DIM), BF16),
        scratch_shapes=[pltpu.VMEM((RET_QK_DIM, RET_V_DIM), F32)],
        compiler_params=_params(("parallel", "parallel", "arbitrary")),
        name="retention",
    )(qk, qk, vg, vg, cos, sin, dmat, qd, kd, bd, gn_g.reshape(1, -1))


def _peer_scores_kernel(x_ref, wq_ref, k1_ref, k2_ref, s1_ref, s2_ref):
    q = jnp.dot(x_ref[...], wq_ref[...], preferred_element_type=F32).astype(BF16)
    nh = s1_ref.shape[0]
    nt = (((1,), (1,)), ((), ()))
    for h in range(nh):
        lo = h * PEER_QUERY_DIM
        s1_ref[h] = lax.dot_general(k1_ref[...], q[:, lo:lo + PEER_HALF], nt,
                                    preferred_element_type=F32)
        s2_ref[h] = lax.dot_general(k2_ref[...], q[:, lo + PEER_HALF:lo + PEER_QUERY_DIM], nt,
                                    preferred_element_type=F32)


def peer_scores(xb, wq, k1, k2, *, tm=256):
    t, d = xb.shape
    nh = wq.shape[1] // PEER_QUERY_DIM
    tm = min(tm, t)
    shp = jax.ShapeDtypeStruct((nh, PEER_N_KEYS, t), F32)
    return pl.pallas_call(
        _peer_scores_kernel,
        grid=(t // tm,),
        in_specs=[pl.BlockSpec((tm, d), lambda i: (i, 0)),
                  pl.BlockSpec(wq.shape, lambda i: (0, 0)),
                  pl.BlockSpec(k1.shape, lambda i: (0, 0)),
                  pl.BlockSpec(k2.shape, lambda i: (0, 0))],
        out_specs=[pl.BlockSpec((nh, PEER_N_KEYS, tm), lambda i: (0, 0, i)),
                   pl.BlockSpec((nh, PEER_N_KEYS, tm), lambda i: (0, 0, i))],
        out_shape=[shp, shp],
        compiler_params=_params(("parallel",)),
        name="peer_scores",
    )(xb, wq, k1, k2)


_PEER_CANDS = tuple((i, j) for i in range(PEER_TOPK) for j in range(PEER_TOPK)
                    if (i + 1) * (j + 1) <= PEER_TOPK)


def _top16_keys(s):
    key = lax.broadcasted_iota(jnp.int32, s.shape, 1)
    vals, idxs = [], []
    for _ in range(PEER_TOPK):
        m = jnp.max(s, axis=1, keepdims=True)
        idx = jnp.min(jnp.where(s == m, key, PEER_N_KEYS), axis=1, keepdims=True)
        s = jnp.where(key == idx, NEG_INF, s)
        vals.append(m[:, 0, :])
        idxs.append(idx[:, 0, :])
    return vals, idxs


def _peer_topk_kernel(s1_ref, s2_ref, eid_ref, g_ref):
    v1, i1 = _top16_keys(s1_ref[...])
    v2, i2 = _top16_keys(s2_ref[...])
    cand = [v1[i] + v2[j] for (i, j) in _PEER_CANDS]
    ceid = [i1[i] * PEER_N_KEYS + i2[j] for (i, j) in _PEER_CANDS]
    ncand = len(_PEER_CANDS)
    scores, eids = [], []
    for _ in range(PEER_TOPK):
        m = functools.reduce(jnp.maximum, cand)
        pos = functools.reduce(jnp.minimum,
                               [jnp.where(cand[c] == m, c, ncand) for c in range(ncand)])
        e = jnp.zeros_like(ceid[0])
        for c in range(ncand):
            hit = pos == c
            e = jnp.where(hit, ceid[c], e)
            cand[c] = jnp.where(hit, NEG_INF, cand[c])
        scores.append(m)
        eids.append(e)
    ex = [jnp.exp(sc - scores[0]) for sc in scores]
    denom = functools.reduce(jnp.add, ex)
    for k in range(PEER_TOPK):
        eid_ref[k] = eids[k]
        g_ref[k] = ex[k] / denom


def peer_topk(s1, s2, *, tt=128):
    nh, nk, t = s1.shape
    tt = min(tt, t)
    return pl.pallas_call(
        _peer_topk_kernel,
        grid=(t // tt,),
        in_specs=[pl.BlockSpec((nh, nk, tt), lambda i: (0, 0, i)),
                  pl.BlockSpec((nh, nk, tt), lambda i: (0, 0, i))],
        out_specs=[pl.BlockSpec((PEER_TOPK, nh, tt), lambda i: (0, 0, i)),
                   pl.BlockSpec((PEER_TOPK, nh, tt), lambda i: (0, 0, i))],
        out_shape=[jax.ShapeDtypeStruct((PEER_TOPK, nh, t), jnp.int32),
                   jax.ShapeDtypeStruct((PEER_TOPK, nh, t), F32)],
        compiler_params=_params(("parallel",)),
        name="peer_topk",
    )(s1, s2)


PEER_GROUP = 16


def _split_bf16(a):
    hi = a.astype(BF16)
    lo = (a - hi.astype(F32)).astype(BF16)
    return jnp.concatenate([hi, lo], axis=0)


def _diag_mask():
    sub = lax.broadcasted_iota(jnp.int32, (V7X_SUBLANES, V7X_LANES), 0)
    lane = lax.broadcasted_iota(jnp.int32, (V7X_SUBLANES, V7X_LANES), 1)
    return (lane & (V7X_SUBLANES - 1)) == sub


PEER_TOKENS_PER_STEP = 4


def _gather_tiles(tab_ref, eid_ref, first):
    grp = eid_ref.at[pl.ds(first, PEER_GROUP)]
    return jnp.concatenate([tab_ref[grp[j]] for j in range(PEER_GROUP)], axis=0)


def _for_each_token(tb, npairs, body):
    def trip(i, carry):
        for u in range(PEER_TOKENS_PER_STEP):
            t = i * PEER_TOKENS_PER_STEP + u
            body(t, t * npairs)
        return carry

    lax.fori_loop(0, tb // PEER_TOKENS_PER_STEP, trip, 0)


def _peer_up_kernel(eid_ref, x_ref, tab_ref, o_ref, zs_ref, *, npairs):
    tb = x_ref.shape[0]
    ngroups = npairs // PEER_GROUP
    diag = _diag_mask()
    nt = (((1,), (1,)), ((), ()))

    def token(t, first):
        lhs = _split_bf16(x_ref[t])
        for k in range(ngroups):
            g = _gather_tiles(tab_ref, eid_ref, first + k * PEER_GROUP)
            z = lax.dot_general(lhs, g, nt, preferred_element_type=F32)
            zs_ref[k, t] = jnp.where(diag, z[:V7X_SUBLANES] + z[V7X_SUBLANES:], 0.0)

    _for_each_token(tb, npairs, token)

    row = lax.broadcasted_iota(jnp.int32, (V7X_LANES, npairs), 0)
    col = lax.broadcasted_iota(jnp.int32, (V7X_LANES, npairs), 1)
    acc = jnp.zeros((tb, npairs), F32)
    for k in range(ngroups):
        zr = jnp.sum(zs_ref[k], axis=1)
        sel = jnp.where(col == k * PEER_GROUP + row // V7X_SUBLANES, 1.0, 0.0).astype(BF16)
        zh = zr.astype(BF16)
        zl = (zr - zh.astype(F32)).astype(BF16)
        acc = acc + jnp.dot(zh, sel, preferred_element_type=F32) + jnp.dot(zl, sel, preferred_element_type=F32)
    o_ref[...] = acc


def peer_up(eid_tok, x4, utab, *, tb=128):
    t, npairs = eid_tok.shape
    nhalf, nexp, r, lanes = utab.shape
    tb = min(tb, t)
    assert r == V7X_SUBLANES and lanes == V7X_LANES and npairs % PEER_GROUP == 0
    return pl.pallas_call(
        functools.partial(_peer_up_kernel, npairs=npairs),
        grid=(nhalf, t // tb),
        in_specs=[pl.BlockSpec((tb * npairs,), lambda c, i: (i,), memory_space=pltpu.SMEM),
                  pl.BlockSpec((tb, None, r, lanes), lambda c, i: (i, c, 0, 0)),
                  pl.BlockSpec((None, nexp, r, lanes), lambda c, i: (c, 0, 0, 0),
                               pipeline_mode=pl.Buffered(1))],
        out_specs=pl.BlockSpec((None, tb, npairs), lambda c, i: (c, i, 0)),
        out_shape=jax.ShapeDtypeStruct((nhalf, t, npairs), F32),
        scratch_shapes=[pltpu.VMEM((npairs // PEER_GROUP, tb, r, lanes), F32)],
        compiler_params=_params(("arbitrary", "arbitrary")),
        name="peer_up",
    )(eid_tok.reshape(-1), x4, utab)


def _peer_act_kernel(h_ref, g_ref, w_ref):
    h = h_ref[0] + h_ref[1]
    gelu = 0.5 * h * (1.0 + lax.erf(h * (2.0 ** -0.5)))
    w_ref[...] = g_ref[...] * gelu


def peer_act(h2, g, *, tt=512):
    _, t, npairs = h2.shape
    tt = min(tt, t)
    return pl.pallas_call(
        _peer_act_kernel,
        grid=(t // tt,),
        in_specs=[pl.BlockSpec((2, tt, npairs), lambda i: (0, i, 0)),
                  pl.BlockSpec((tt, npairs), lambda i: (i, 0))],
        out_specs=pl.BlockSpec((tt, npairs), lambda i: (i, 0)),
        out_shape=jax.ShapeDtypeStruct((t, npairs), F32),
        compiler_params=_params(("parallel",)),
        name="peer_act",
    )(h2, g)


def _peer_down_kernel(eid_ref, w_ref, tab_ref, o_ref, *, npairs):
    tb = o_ref.shape[0]
    ngroups = npairs // PEER_GROUP
    diag = _diag_mask()

    def token(t, first):
        wall = w_ref[t]
        acc = jnp.zeros((2 * V7X_SUBLANES, V7X_LANES), F32)
        for k in range(ngroups):
            g = _gather_tiles(tab_ref, eid_ref, first + k * PEER_GROUP)
            wk = jnp.broadcast_to(wall[k:k + 1, :], (V7X_SUBLANES, V7X_LANES))
            lhs = _split_bf16(jnp.where(diag, wk, 0.0))
            acc = acc + jnp.dot(lhs, g, preferred_element_type=F32)
        o_ref[t] = acc[:V7X_SUBLANES] + acc[V7X_SUBLANES:]

    _for_each_token(tb, npairs, token)


def peer_down(eid_tok, w_rep, vtab, *, tb=128):
    t, npairs = eid_tok.shape
    nhalf, nexp, r, lanes = vtab.shape
    tb = min(tb, t)
    assert r == V7X_SUBLANES and lanes == V7X_LANES and npairs // PEER_GROUP == V7X_SUBLANES
    return pl.pallas_call(
        functools.partial(_peer_down_kernel, npairs=npairs),
        grid=(nhalf, t // tb),
        in_specs=[pl.BlockSpec((tb * npairs,), lambda c, i: (i,), memory_space=pltpu.SMEM),
                  pl.BlockSpec((tb, npairs // PEER_GROUP, lanes), lambda c, i: (i, 0, 0)),
                  pl.BlockSpec((None, nexp, r, lanes), lambda c, i: (c, 0, 0, 0),
                               pipeline_mode=pl.Buffered(1))],
        out_specs=pl.BlockSpec((tb, None, r, lanes), lambda c, i: (i, c, 0, 0)),
        out_shape=jax.ShapeDtypeStruct((t, nhalf, r, lanes), F32),
        compiler_params=_params(("arbitrary", "arbitrary")),
        name="peer_down",
    )(eid_tok.reshape(-1), w_rep, vtab)


def _expert_table(w):
    nexp, d = w.shape
    r = d // (2 * V7X_LANES)
    return jnp.transpose(w.astype(BF16).reshape(nexp, 2, r, V7X_LANES), (1, 0, 2, 3))


def peer_ffn(x_f32, x_bf16, wq, k1, k2, u, v):
    t, d = x_f32.shape
    r = d // (2 * V7X_LANES)
    s1, s2 = peer_scores(x_bf16, wq.astype(BF16), k1.astype(BF16), k2.astype(BF16))
    eid, gate = peer_topk(s1, s2)
    npairs = eid.shape[0] * eid.shape[1]
    eid_tok = eid.reshape(npairs, t).T
    h2 = peer_up(eid_tok, x_f32.reshape(t, 2, r, V7X_LANES), _expert_table(u))
    w = peer_act(h2, gate.reshape(npairs, t).T)
    w_rep = jnp.repeat(w, V7X_SUBLANES, axis=1).reshape(t, npairs // PEER_GROUP, V7X_LANES)
    y = peer_down(eid_tok, w_rep, _expert_table(v))
    return y.reshape(t, d)


def fox_layer(x_f32, x_bf16, bsz, s, w_in, b_f, w_o, ln_g, ln_b):
    t, d = x_f32.shape
    nh = d // FOX_HEAD_DIM
    wb = w_in.astype(BF16)
    q = matmul(x_bf16, wb[:, :d], BF16, scale=LOG2E * FOX_HEAD_DIM ** -0.5, name="fox_q")
    kv = matmul(x_bf16, wb[:, d:3 * d], BF16, name="fox_kv")
    wg = jnp.pad(wb[:, 3 * d:], ((0, 0), (0, V7X_LANES - nh)))
    gl = matmul(x_bf16, wg, F32, tn=V7X_LANES, name="fox_gate")
    bf_row = jnp.pad(b_f.astype(F32), (0, V7X_LANES - nh)).reshape(1, V7X_LANES)
    c = forget_cumsum(gl.reshape(bsz, s, V7X_LANES), bf_row)[:, :, :nh]
    ct = jnp.transpose(c, (0, 2, 1))
    o = fox_attention(q.reshape(bsz, s, d), kv.reshape(bsz, s, 2 * d),
                      ct[..., None], ct[:, :, None, :])
    return proj_residual_ln(o.reshape(t, d), w_o.astype(BF16), x_f32, ln_g, ln_b, name="fox_out_ln")


def retention_layer(x_f32, x_bf16, bsz, s, w_in, gn_g, w_o, ln_g, ln_b):
    t, d = x_f32.shape
    nh = d // RET_QK_DIM
    qkw = 2 * nh * RET_QK_DIM
    wb = w_in.astype(BF16)
    qk = matmul(x_bf16, wb[:, :qkw], F32, name="ret_qk")
    vg = matmul(x_bf16, wb[:, qkw:], BF16, name="ret_vg")
    half = RET_QK_DIM // 2
    inv_freq = ROPE_BASE ** (-jnp.arange(half, dtype=F32) / half)
    ang = jnp.arange(s, dtype=F32)[:, None] * inv_freq[None, :]
    o = retention(qk.reshape(bsz, s, -1), vg.reshape(bsz, s, -1), jnp.cos(ang), jnp.sin(ang), gn_g)
    return proj_residual_ln(o.reshape(t, -1), w_o.astype(BF16), x_f32, ln_g, ln_b, name="ret_out_ln")


def kernel(x, l0_fox_w_in, l0_fox_b_f, l0_fox_w_o, l0_ln1_g, l0_ln1_b, l0_peer_wq, l0_peer_k1, l0_peer_k2, l0_peer_u, l0_peer_v, l0_ln2_g, l0_ln2_b, l1_ret_w_in, l1_ret_gn_g, l1_ret_w_o, l1_ln1_g, l1_ln1_b, l1_peer_wq, l1_peer_k1, l1_peer_k2, l1_peer_u, l1_peer_v, l1_ln2_g, l1_ln2_b):
    bsz, s, d = x.shape
    t = bsz * s
    xf = x.reshape(t, d).astype(F32)
    xb = xf.astype(BF16)

    xf, xb = fox_layer(xf, xb, bsz, s, l0_fox_w_in, l0_fox_b_f, l0_fox_w_o, l0_ln1_g, l0_ln1_b)
    y = peer_ffn(xf, xb, l0_peer_wq, l0_peer_k1, l0_peer_k2, l0_peer_u, l0_peer_v)
    xf, xb = residual_ln(y, xf, l0_ln2_g, l0_ln2_b, name="peer0_ln")

    xf, xb = retention_layer(xf, xb, bsz, s, l1_ret_w_in, l1_ret_gn_g, l1_ret_w_o, l1_ln1_g, l1_ln1_b)
    y = peer_ffn(xf, xb, l1_peer_wq, l1_peer_k1, l1_peer_k2, l1_peer_u, l1_peer_v)
    xf, _ = residual_ln(y, xf, l1_ln2_g, l1_ln2_b, name="peer1_ln")
    return xf.reshape(bsz, s, d).astype(x.dtype)
```

```python
import functools
import math

import jax
import jax.numpy as jnp
from jax import lax
from jax.experimental import pallas as pl
from jax.experimental.pallas import tpu as pltpu

DEPTH = 2
CHUNK = 64
FOX_HEAD_DIM = 128
RET_QK_DIM = 256
RET_V_DIM = 512
ROPE_BASE = 10000.0
PEER_N_KEYS = 128
PEER_QUERY_DIM = 256
PEER_HALF = PEER_QUERY_DIM // 2
PEER_TOPK = 16
DN_ALPHA = (2 * DEPTH) ** 0.25
LN_EPS = 1e-5

V7X_LANES = 128
V7X_SUBLANES = 8
V7X_VMEM_BYTES = 64 * 1024 * 1024
VMEM_LIMIT = V7X_VMEM_BYTES - 6 * 1024 * 1024

F32 = jnp.float32
BF16 = jnp.bfloat16
NEG_INF = float("-inf")


def _params(sem, vmem=None):
    return pltpu.CompilerParams(dimension_semantics=sem, vmem_limit_bytes=vmem or VMEM_LIMIT)


def _mm_kernel(a_ref, b_ref, o_ref, *, scale):
    acc = jnp.dot(a_ref[...], b_ref[...], preferred_element_type=F32)
    if scale is not None:
        acc = acc * scale
    o_ref[...] = acc.astype(o_ref.dtype)


def matmul(a, b, out_dtype, *, scale=None, tm=1024, tn=1024, name="mm"):
    m, k = a.shape
    _, n = b.shape
    tm = min(tm, m)
    tn = min(tn, n)
    return pl.pallas_call(
        functools.partial(_mm_kernel, scale=scale),
        grid=(n // tn, m // tm),
        in_specs=[pl.BlockSpec((tm, k), lambda j, i: (i, 0)),
                  pl.BlockSpec((k, tn), lambda j, i: (0, j))],
        out_specs=pl.BlockSpec((tm, tn), lambda j, i: (i, j)),
        out_shape=jax.ShapeDtypeStruct((m, n), out_dtype),
        compiler_params=_params(("parallel", "parallel")),
        name=name,
    )(a, b)


def _layer_norm_rows(y, g, b):
    mu = jnp.mean(y, axis=-1, keepdims=True)
    yc = y - mu
    var = jnp.mean(yc * yc, axis=-1, keepdims=True)
    return yc * lax.rsqrt(var + LN_EPS) * g + b


def _proj_ln_kernel(a_ref, w_ref, x_ref, g_ref, b_ref, of_ref, ob_ref):
    y = jnp.dot(a_ref[...], w_ref[...], preferred_element_type=F32) + DN_ALPHA * x_ref[...]
    yn = _layer_norm_rows(y, g_ref[...], b_ref[...])
    of_ref[...] = yn
    ob_ref[...] = yn.astype(BF16)


def proj_residual_ln(a, w, x, g, b, *, tm=256, name="proj_ln"):
    m, k = a.shape
    d = w.shape[1]
    tm = min(tm, m)
    return pl.pallas_call(
        _proj_ln_kernel,
        grid=(m // tm,),
        in_specs=[pl.BlockSpec((tm, k), lambda i: (i, 0)),
                  pl.BlockSpec((k, d), lambda i: (0, 0)),
                  pl.BlockSpec((tm, d), lambda i: (i, 0)),
                  pl.BlockSpec((1, d), lambda i: (0, 0)),
                  pl.BlockSpec((1, d), lambda i: (0, 0))],
        out_specs=[pl.BlockSpec((tm, d), lambda i: (i, 0)),
                   pl.BlockSpec((tm, d), lambda i: (i, 0))],
        out_shape=[jax.ShapeDtypeStruct((m, d), F32), jax.ShapeDtypeStruct((m, d), BF16)],
        compiler_params=_params(("parallel",)),
        name=name,
    )(a, w, x, g.reshape(1, d), b.reshape(1, d))


def _residual_ln_kernel(y_ref, x_ref, g_ref, b_ref, of_ref, ob_ref):
    y = y_ref[...] + DN_ALPHA * x_ref[...]
    yn = _layer_norm_rows(y, g_ref[...], b_ref[...])
    of_ref[...] = yn
    ob_ref[...] = yn.astype(BF16)


def residual_ln(y, x, g, b, *, tm=512, name="res_ln"):
    m, d = x.shape
    tm = min(tm, m)
    return pl.pallas_call(
        _residual_ln_kernel,
        grid=(m // tm,),
        in_specs=[pl.BlockSpec((tm, d), lambda i: (i, 0)),
                  pl.BlockSpec((tm, d), lambda i: (i, 0)),
                  pl.BlockSpec((1, d), lambda i: (0, 0)),
                  pl.BlockSpec((1, d), lambda i: (0, 0))],
        out_specs=[pl.BlockSpec((tm, d), lambda i: (i, 0)),
                   pl.BlockSpec((tm, d), lambda i: (i, 0))],
        out_shape=[jax.ShapeDtypeStruct((m, d), F32), jax.ShapeDtypeStruct((m, d), BF16)],
        compiler_params=_params(("parallel",)),
        name=name,
    )(y, x, g.reshape(1, d), b.reshape(1, d))


def _forget_cumsum_kernel(g_ref, bf_ref, c_ref, carry_ref):
    @pl.when(pl.program_id(1) == 0)
    def _():
        carry_ref[...] = jnp.zeros_like(carry_ref)

    z = g_ref[0] + bf_ref[...]
    lf = jnp.minimum(z, 0.0) - jnp.log1p(jnp.exp(-jnp.abs(z)))
    tc = lf.shape[0]
    row = lax.broadcasted_iota(jnp.int32, (tc, tc), 0)
    col = lax.broadcasted_iota(jnp.int32, (tc, tc), 1)
    tri = jnp.where(row >= col, 1.0, 0.0).astype(BF16)
    hi = lf.astype(BF16)
    lo = (lf - hi.astype(F32)).astype(BF16)
    cs = (jnp.dot(tri, hi, preferred_element_type=F32)
          + jnp.dot(tri, lo, preferred_element_type=F32))
    c = cs + carry_ref[...]
    c_ref[0] = c
    carry_ref[...] = c[tc - 1:tc, :]


def forget_cumsum(gate_logits, b_f_row, *, tc=512):
    bsz, s, w = gate_logits.shape
    tc = min(tc, s)
    return pl.pallas_call(
        _forget_cumsum_kernel,
        grid=(bsz, s // tc),
        in_specs=[pl.BlockSpec((1, tc, w), lambda b, i: (b, i, 0)),
                  pl.BlockSpec((1, w), lambda b, i: (0, 0))],
        out_specs=pl.BlockSpec((1, tc, w), lambda b, i: (b, i, 0)),
        out_shape=jax.ShapeDtypeStruct((bsz, s, w), F32),
        scratch_shapes=[pltpu.VMEM((1, w), F32)],
        compiler_params=_params(("parallel", "arbitrary")),
        name="forget_cumsum",
    )(gate_logits, b_f_row)


LOG2E = math.log2(math.e)


FOX_HEADS_PER_STEP = 2


FOX_AUG_DIM = 2 * FOX_HEAD_DIM
FOX_BIAS_TERMS = 3


def _fox_kernel(qi_ref, ki_ref, q_ref, k_ref, vt_ref, o_ref, m_ref, l_ref, acc_ref, *, tb):
    step_id = pl.program_id(2)
    qi = qi_ref[step_id]
    ki = ki_ref[step_id]
    hd = FOX_HEAD_DIM

    @pl.when(ki == 0)
    def _():
        m_ref[...] = jnp.full_like(m_ref, NEG_INF)
        l_ref[...] = jnp.zeros_like(l_ref)
        acc_ref[...] = jnp.zeros_like(acc_ref)

    def step(masked, last):
        for g in range(FOX_HEADS_PER_STEP):
            cols = slice(g * FOX_AUG_DIM, (g + 1) * FOX_AUG_DIM)
            rows = slice(g * hd, (g + 1) * hd)
            st = lax.dot_general(k_ref[0, :, cols], q_ref[0, :, cols], (((1,), (1,)), ((), ())),
                                 preferred_element_type=F32)
            if masked:
                key = lax.broadcasted_iota(jnp.int32, (tb, tb), 0)
                qry = lax.broadcasted_iota(jnp.int32, (tb, tb), 1)
                st = jnp.where(key <= qry, st, NEG_INF)
            m_prev = m_ref[g]
            m_new = jnp.maximum(m_prev, jnp.max(st, axis=0, keepdims=True))
            alpha = jnp.exp2(m_prev - m_new)
            p = jnp.exp2(st - m_new)
            l_new = alpha * l_ref[g] + jnp.sum(p, axis=0, keepdims=True)
            acc_new = alpha * acc_ref[g] + jnp.dot(vt_ref[rows, :], p.astype(BF16),
                                                   preferred_element_type=F32)
            if last:
                o_ref[rows, :] = (acc_new / l_new).astype(o_ref.dtype)
            else:
                m_ref[g] = m_new
                l_ref[g] = l_new
                acc_ref[g] = acc_new

    @pl.when(ki < qi)
    def _():
        step(False, False)

    @pl.when(ki == qi)
    def _():
        step(True, True)


def fox_attention(q_aug, k_aug, vt, *, tb=512):
    bsz, s, w = q_aug.shape
    nh = w // FOX_AUG_DIM
    hps = FOX_HEADS_PER_STEP
    assert nh % hps == 0
    tb = min(tb, s)
    nb = s // tb
    pairs = [(qi, ki) for qi in range(nb) for ki in range(qi + 1)]
    qi_arr = jnp.asarray([p[0] for p in pairs], jnp.int32)
    ki_arr = jnp.asarray([p[1] for p in pairs], jnp.int32)
    grid_spec = pltpu.PrefetchScalarGridSpec(
        num_scalar_prefetch=2,
        grid=(bsz, nh // hps, len(pairs)),
        in_specs=[
            pl.BlockSpec((1, tb, hps * FOX_AUG_DIM), lambda b, h, i, qa, ka: (b, qa[i], h)),
            pl.BlockSpec((1, tb, hps * FOX_AUG_DIM), lambda b, h, i, qa, ka: (b, ka[i], h)),
            pl.BlockSpec((hps * FOX_HEAD_DIM, tb), lambda b, h, i, qa, ka: (h, b * nb + ka[i])),
        ],
        out_specs=pl.BlockSpec((hps * FOX_HEAD_DIM, tb), lambda b, h, i, qa, ka: (h, b * nb + qa[i])),
        scratch_shapes=[pltpu.VMEM((hps, 1, tb), F32), pltpu.VMEM((hps, 1, tb), F32),
                        pltpu.VMEM((hps, FOX_HEAD_DIM, tb), F32)],
    )
    return pl.pallas_call(
        functools.partial(_fox_kernel, tb=tb),
        grid_spec=grid_spec,
        out_shape=jax.ShapeDtypeStruct(vt.shape, BF16),
        compiler_params=_params(("parallel", "parallel", "arbitrary")),
        name="fox_attention",
    )(qi_arr, ki_arr, q_aug, k_aug, vt)


def _rope_rows(a, cos, sin):
    half = a.shape[-1] // 2
    a1, a2 = a[:, :half], a[:, half:]
    return jnp.concatenate([a1 * cos - a2 * sin, a1 * sin + a2 * cos], axis=-1)


def _retention_kernel(q_ref, k_ref, v_ref, gate_ref, cos_ref, sin_ref, dmat_ref, qd_ref, kd_ref,
                      bd_ref, gn_ref, o_ref, state_ref):
    @pl.when(pl.program_id(2) == 0)
    def _():
        state_ref[...] = jnp.zeros_like(state_ref)

    cos = cos_ref[...]
    sin = sin_ref[...]
    q = _rope_rows(q_ref[0], cos, sin)
    k = _rope_rows(k_ref[0], cos, sin) * (RET_QK_DIM ** -0.5)
    v = v_ref[0]
    scores = lax.dot_general(q.astype(BF16), k.astype(BF16), (((1,), (1,)), ((), ())),
                             preferred_element_type=F32) * dmat_ref[0]
    out = jnp.dot(scores.astype(BF16), v, preferred_element_type=F32)
    state = state_ref[...]
    out = out + jnp.dot((q * qd_ref[0]).astype(BF16), state.astype(BF16),
                        preferred_element_type=F32)
    kdt = jnp.transpose(k * kd_ref[0]).astype(BF16)
    state_ref[...] = state * bd_ref[0] + jnp.dot(kdt, v, preferred_element_type=F32)
    mu = jnp.mean(out, axis=-1, keepdims=True)
    oc = out - mu
    var = jnp.mean(oc * oc, axis=-1, keepdims=True)
    o = oc * lax.rsqrt(var + LN_EPS) * gn_ref[...]
    g = gate_ref[0].astype(F32)
    o_ref[0] = (g * jax.nn.sigmoid(g) * o).astype(o_ref.dtype)


def _retention_tables(nh, blk):
    log_gamma = jnp.log(1.0 - jnp.exp2(-5.0 - jnp.arange(nh, dtype=F32)))
    t = jnp.arange(blk, dtype=F32)
    chunk_id = jnp.arange(blk) // CHUNK
    dist = jnp.abs(t[:, None] - t[None, :])
    allowed = chunk_id[None, :] <= chunk_id[:, None]
    dmat = jnp.where(allowed[None], jnp.exp(log_gamma[:, None, None] * dist[None]), 0.0)
    qd = jnp.exp(log_gamma[:, None] * (t[None, :] + 1.0))[..., None]
    kd = jnp.exp(log_gamma[:, None] * (blk - 1.0 - t[None, :]))[..., None]
    bd = jnp.exp(log_gamma * blk).reshape(nh, 1, 1)
    return dmat.astype(F32), qd, kd, bd


def retention(qk, vg, cos, sin, gn_g, *, blk=256):
    bsz, s, w = qk.shape
    nh = w // (2 * RET_QK_DIM)
    blk = min(blk, s)
    dmat, qd, kd, bd = _retention_tables(nh, blk)
    half = RET_QK_DIM // 2
    return pl.pallas_call(
        _retention_kernel,
        grid=(bsz, nh, s // blk),
        in_specs=[
            pl.BlockSpec((1, blk, RET_QK_DIM), lambda b, h, i: (b, i, h)),
            pl.BlockSpec((1, blk, RET_QK_DIM), lambda b, h, i: (b, i, nh + h)),
            pl.BlockSpec((1, blk, RET_V_DIM), lambda b, h, i: (b, i, h)),
            pl.BlockSpec((1, blk, RET_V_DIM), lambda b, h, i: (b, i, nh + h)),
            pl.BlockSpec((blk, half), lambda b, h, i: (i, 0)),
            pl.BlockSpec((blk, half), lambda b, h, i: (i, 0)),
            pl.BlockSpec((1, blk, blk), lambda b, h, i: (h, 0, 0)),
            pl.BlockSpec((1, blk, 1), lambda b, h, i: (h, 0, 0)),
            pl.BlockSpec((1, blk, 1), lambda b, h, i: (h, 0, 0)),
            pl.BlockSpec((1, 1, 1), lambda b, h, i: (h, 0, 0)),
            pl.BlockSpec((1, RET_V_DIM), lambda b, h, i: (0, h)),
        ],
        out_specs=pl.BlockSpec((1, blk, RET_V_DIM), lambda b, h, i: (b, i, h)),
        out_shape=jax.ShapeDtypeStruct((bsz, s, nh * RET_V_DIM), BF16),
        scratch_shapes=[pltpu.VMEM((RET_QK_DIM, RET_V_DIM), F32)],
        compiler_params=_params(("parallel", "parallel", "arbitrary")),
        name="retention",
    )(qk, qk, vg, vg, cos, sin, dmat, qd, kd, bd, gn_g.reshape(1, -1))


def _peer_scores_kernel(x_ref, wq_ref, k1_ref, k2_ref, s1_ref, s2_ref):
    q = jnp.dot(x_ref[...], wq_ref[...], preferred_element_type=F32).astype(BF16)
    nh = s1_ref.shape[0]
    nt = (((1,), (1,)), ((), ()))
    for h in range(nh):
        lo = h * PEER_QUERY_DIM
        s1_ref[h] = lax.dot_general(k1_ref[...], q[:, lo:lo + PEER_HALF], nt,
                                    preferred_element_type=F32)
        s2_ref[h] = lax.dot_general(k2_ref[...], q[:, lo + PEER_HALF:lo + PEER_QUERY_DIM], nt,
                                    preferred_element_type=F32)


def peer_scores(xb, wq, k1, k2, *, tm=256):
    t, d = xb.shape
    nh = wq.shape[1] // PEER_QUERY_DIM
    tm = min(tm, t)
    shp = jax.ShapeDtypeStruct((nh, PEER_N_KEYS, t), F32)
    return pl.pallas_call(
        _peer_scores_kernel,
        grid=(t // tm,),
        in_specs=[pl.BlockSpec((tm, d), lambda i: (i, 0)),
                  pl.BlockSpec(wq.shape, lambda i: (0, 0)),
                  pl.BlockSpec(k1.shape, lambda i: (0, 0)),
                  pl.BlockSpec(k2.shape, lambda i: (0, 0))],
        out_specs=[pl.BlockSpec((nh, PEER_N_KEYS, tm), lambda i: (0, 0, i)),
                   pl.BlockSpec((nh, PEER_N_KEYS, tm), lambda i: (0, 0, i))],
        out_shape=[shp, shp],
        compiler_params=_params(("parallel",)),
        name="peer_scores",
    )(xb, wq, k1, k2)


_PEER_CANDS = tuple((i, j) for i in range(PEER_TOPK) for j in range(PEER_TOPK)
                    if (i + 1) * (j + 1) <= PEER_TOPK)


def _top16_keys(s):
    nh, _, tt = s.shape
    key = lax.broadcasted_iota(jnp.int32, s.shape, 1)
    sub = lax.broadcasted_iota(jnp.int32, (nh, tt), 0)

    def pack_heads(a):
        out = jnp.broadcast_to(a[0], (nh, tt))
        for h in range(1, nh):
            out = jnp.where(sub == h, jnp.broadcast_to(a[h], (nh, tt)), out)
        return out

    vals, idxs = [], []
    for _ in range(PEER_TOPK):
        m = jnp.max(s, axis=1, keepdims=True)
        idx = jnp.min(jnp.where(s == m, key, PEER_N_KEYS), axis=1, keepdims=True)
        s = jnp.where(key == idx, NEG_INF, s)
        vals.append(pack_heads(m))
        idxs.append(pack_heads(idx))
    return vals, idxs


def _peer_topk_kernel(s1_ref, s2_ref, eid_ref, g_ref):
    v1, i1 = _top16_keys(s1_ref[...])
    v2, i2 = _top16_keys(s2_ref[...])
    cand = [v1[i] + v2[j] for (i, j) in _PEER_CANDS]
    ceid = [i1[i] * PEER_N_KEYS + i2[j] for (i, j) in _PEER_CANDS]
    ncand = len(_PEER_CANDS)
    scores, eids = [], []
    for _ in range(PEER_TOPK):
        m = functools.reduce(jnp.maximum, cand)
        pos = functools.reduce(jnp.minimum,
                               [jnp.where(cand[c] == m, c, ncand) for c in range(ncand)])
        e = jnp.zeros_like(ceid[0])
        for c in range(ncand):
            hit = pos == c
            e = jnp.where(hit, ceid[c], e)
            cand[c] = jnp.where(hit, NEG_INF, cand[c])
        scores.append(m)
        eids.append(e)
    ex = [jnp.exp(sc - scores[0]) for sc in scores]
    denom = functools.reduce(jnp.add, ex)
    for k in range(PEER_TOPK):
        eid_ref[k] = eids[k]
        g_ref[k] = ex[k] / denom


def peer_topk(s1, s2, *, tt=128):
    nh, nk, t = s1.shape
    tt = min(tt, t)
    return pl.pallas_call(
        _peer_topk_kernel,
        grid=(t // tt,),
        in_specs=[pl.BlockSpec((nh, nk, tt), lambda i: (0, 0, i)),
                  pl.BlockSpec((nh, nk, tt), lambda i: (0, 0, i))],
        out_specs=[pl.BlockSpec((PEER_TOPK, nh, tt), lambda i: (0, 0, i)),
                   pl.BlockSpec((PEER_TOPK, nh, tt), lambda i: (0, 0, i))],
        out_shape=[jax.ShapeDtypeStruct((PEER_TOPK, nh, t), jnp.int32),
                   jax.ShapeDtypeStruct((PEER_TOPK, nh, t), F32)],
        compiler_params=_params(("parallel",)),
        name="peer_topk",
    )(s1, s2)


PEER_GROUP = 16


def _split_bf16(a):
    hi = a.astype(BF16)
    lo = (a - hi.astype(F32)).astype(BF16)
    return jnp.concatenate([hi, lo], axis=0)


def _diag_mask():
    sub = lax.broadcasted_iota(jnp.int32, (V7X_SUBLANES, V7X_LANES), 0)
    lane = lax.broadcasted_iota(jnp.int32, (V7X_SUBLANES, V7X_LANES), 1)
    return (lane & (V7X_SUBLANES - 1)) == sub


PEER_TOKENS_PER_STEP = 8


def _gather_tiles(tab_ref, eid_ref, first):
    grp = eid_ref.at[pl.ds(first, PEER_GROUP)]
    return jnp.concatenate([tab_ref[grp[j]] for j in range(PEER_GROUP)], axis=0)


def _for_each_token(tb, npairs, body):
    def trip(i, carry):
        for u in range(PEER_TOKENS_PER_STEP):
            t = i * PEER_TOKENS_PER_STEP + u
            body(t, t * npairs)
        return carry

    lax.fori_loop(0, tb // PEER_TOKENS_PER_STEP, trip, 0)


def _peer_up_kernel(eid_ref, x_ref, tab_ref, o_ref, zs_ref, *, npairs):
    tb = x_ref.shape[0]
    ngroups = npairs // PEER_GROUP
    diag = _diag_mask()
    nt = (((1,), (1,)), ((), ()))

    sub = lax.broadcasted_iota(jnp.int32, (V7X_SUBLANES, V7X_LANES), 0)

    def trip(i, carry):
        tiles = [jnp.zeros((V7X_SUBLANES, V7X_LANES), F32) for _ in range(ngroups)]
        for u in range(V7X_SUBLANES):
            t = i * V7X_SUBLANES + u
            lhs = _split_bf16(x_ref[t])
            for k in range(ngroups):
                g = _gather_tiles(tab_ref, eid_ref, t * npairs + k * PEER_GROUP)
                z = lax.dot_general(lhs, g, nt, preferred_element_type=F32)
                zd = jnp.where(diag, z[:V7X_SUBLANES] + z[V7X_SUBLANES:], 0.0)
                for shift in (1, 2, 4):
                    zd = zd + pltpu.roll(zd, shift=shift, axis=0)
                tiles[k] = jnp.where(sub == u, zd, tiles[k])
        for k in range(ngroups):
            zs_ref[k, i] = tiles[k]
        return carry

    lax.fori_loop(0, tb // V7X_SUBLANES, trip, 0)

    row = lax.broadcasted_iota(jnp.int32, (V7X_LANES, npairs), 0)
    col = lax.broadcasted_iota(jnp.int32, (V7X_LANES, npairs), 1)
    acc = jnp.zeros((tb, npairs), F32)
    for k in range(ngroups):
        zr = zs_ref[k].reshape(tb, V7X_LANES)
        sel = jnp.where(col == k * PEER_GROUP + row // V7X_SUBLANES, 1.0, 0.0).astype(BF16)
        zh = zr.astype(BF16)
        zl = (zr - zh.astype(F32)).astype(BF16)
        acc = acc + jnp.dot(zh, sel, preferred_element_type=F32) + jnp.dot(zl, sel, preferred_element_type=F32)
    o_ref[...] = acc


def peer_up(eid_tok, x4, utab, *, tb=128):
    t, npairs = eid_tok.shape
    nhalf, nexp, r, lanes = utab.shape
    tb = min(tb, t)
    assert r == V7X_SUBLANES and lanes == V7X_LANES and npairs % PEER_GROUP == 0
    return pl.pallas_call(
        functools.partial(_peer_up_kernel, npairs=npairs),
        grid=(nhalf, t // tb),
        in_specs=[pl.BlockSpec((tb * npairs,), lambda c, i: (i,), memory_space=pltpu.SMEM),
                  pl.BlockSpec((tb, None, r, lanes), lambda c, i: (i, c, 0, 0)),
                  pl.BlockSpec((None, nexp, r, lanes), lambda c, i: (c, 0, 0, 0),
                               pipeline_mode=pl.Buffered(1))],
        out_specs=pl.BlockSpec((None, tb, npairs), lambda c, i: (c, i, 0)),
        out_shape=jax.ShapeDtypeStruct((nhalf, t, npairs), F32),
        scratch_shapes=[pltpu.VMEM((npairs // PEER_GROUP, tb // V7X_SUBLANES, r, lanes), F32)],
        compiler_params=_params(("arbitrary", "arbitrary")),
        name="peer_up",
    )(eid_tok.reshape(-1), x4, utab)


def _peer_act_kernel(h_ref, g_ref, w_ref):
    h = h_ref[0] + h_ref[1]
    gelu = 0.5 * h * (1.0 + lax.erf(h * (2.0 ** -0.5)))
    w_ref[...] = g_ref[...] * gelu


def peer_act(h2, g, *, tt=512):
    _, t, npairs = h2.shape
    tt = min(tt, t)
    return pl.pallas_call(
        _peer_act_kernel,
        grid=(t // tt,),
        in_specs=[pl.BlockSpec((2, tt, npairs), lambda i: (0, i, 0)),
                  pl.BlockSpec((tt, npairs), lambda i: (i, 0))],
        out_specs=pl.BlockSpec((tt, npairs), lambda i: (i, 0)),
        out_shape=jax.ShapeDtypeStruct((t, npairs), F32),
        compiler_params=_params(("parallel",)),
        name="peer_act",
    )(h2, g)


def _peer_down_kernel(eid_ref, w_ref, tab_ref, o_ref, *, npairs):
    tb = o_ref.shape[0]
    ngroups = npairs // PEER_GROUP
    diag = _diag_mask()

    def token(t, first):
        wall = w_ref[t]
        acc = jnp.zeros((2 * V7X_SUBLANES, V7X_LANES), F32)
        for k in range(ngroups):
            g = _gather_tiles(tab_ref, eid_ref, first + k * PEER_GROUP)
            wk = jnp.broadcast_to(wall[k:k + 1, :], (V7X_SUBLANES, V7X_LANES))
            lhs = _split_bf16(jnp.where(diag, wk, 0.0))
            acc = acc + jnp.dot(lhs, g, preferred_element_type=F32)
        o_ref[t] = acc[:V7X_SUBLANES] + acc[V7X_SUBLANES:]

    _for_each_token(tb, npairs, token)


def peer_down(eid_tok, w_rep, vtab, *, tb=128):
    t, npairs = eid_tok.shape
    nhalf, nexp, r, lanes = vtab.shape
    tb = min(tb, t)
    assert r == V7X_SUBLANES and lanes == V7X_LANES and npairs // PEER_GROUP == V7X_SUBLANES
    return pl.pallas_call(
        functools.partial(_peer_down_kernel, npairs=npairs),
        grid=(nhalf, t // tb),
        in_specs=[pl.BlockSpec((tb * npairs,), lambda c, i: (i,), memory_space=pltpu.SMEM),
                  pl.BlockSpec((tb, npairs // PEER_GROUP, lanes), lambda c, i: (i, 0, 0)),
                  pl.BlockSpec((None, nexp, r, lanes), lambda c, i: (c, 0, 0, 0),
                               pipeline_mode=pl.Buffered(1))],
        out_specs=pl.BlockSpec((tb, None, r, lanes), lambda c, i: (i, c, 0, 0)),
        out_shape=jax.ShapeDtypeStruct((t, nhalf, r, lanes), F32),
        compiler_params=_params(("arbitrary", "arbitrary")),
        name="peer_down",
    )(eid_tok.reshape(-1), w_rep, vtab)


def _expert_table(w):
    nexp, d = w.shape
    r = d // (2 * V7X_LANES)
    return jnp.transpose(w.astype(BF16).reshape(nexp, 2, r, V7X_LANES), (1, 0, 2, 3))


def peer_ffn(x_f32, x_bf16, wq, k1, k2, u, v):
    t, d = x_f32.shape
    r = d // (2 * V7X_LANES)
    s1, s2 = peer_scores(x_bf16, wq.astype(BF16), k1.astype(BF16), k2.astype(BF16))
    eid, gate = peer_topk(s1, s2)
    npairs = eid.shape[0] * eid.shape[1]
    eid_tok = eid.reshape(npairs, t).T
    h2 = peer_up(eid_tok, x_f32.reshape(t, 2, r, V7X_LANES), _expert_table(u))
    w = peer_act(h2, gate.reshape(npairs, t).T)
    w_rep = jnp.repeat(w, V7X_SUBLANES, axis=1).reshape(t, npairs // PEER_GROUP, V7X_LANES)
    y = peer_down(eid_tok, w_rep, _expert_table(v))
    return y.reshape(t, d)


def fox_layer(x_f32, x_bf16, bsz, s, w_in, b_f, w_o, ln_g, ln_b):
    t, d = x_f32.shape
    nh = d // FOX_HEAD_DIM
    wb = w_in.astype(BF16)
    q2 = matmul(x_bf16, wb[:, :d], BF16, scale=LOG2E * FOX_HEAD_DIM ** -0.5, name="fox_q")
    k = matmul(x_bf16, wb[:, d:2 * d], BF16, name="fox_k")
    vt = matmul(wb[:, 2 * d:3 * d].T, x_bf16.T, BF16, name="fox_vt")
    wg = jnp.pad(wb[:, 3 * d:], ((0, 0), (0, V7X_LANES - nh)))
    gl = matmul(x_bf16, wg, F32, tn=V7X_LANES, name="fox_gate")
    bf_row = jnp.pad(b_f.astype(F32), (0, V7X_LANES - nh)).reshape(1, V7X_LANES)
    c = forget_cumsum(gl.reshape(bsz, s, V7X_LANES), bf_row)[:, :, :nh]
    c2 = (c * LOG2E).reshape(t, nh)
    terms, rest = [], c2
    for _ in range(FOX_BIAS_TERMS):
        part = rest.astype(BF16)
        terms.append(part)
        rest = rest - part.astype(F32)
    cterms = jnp.stack(terms, axis=-1)
    ones = jnp.ones_like(cterms)
    pad = jnp.zeros((t, nh, FOX_AUG_DIM - FOX_HEAD_DIM - 2 * FOX_BIAS_TERMS), BF16)
    q_aug = jnp.concatenate([q2.reshape(t, nh, FOX_HEAD_DIM), cterms, ones, pad],
                            axis=-1).reshape(bsz, s, nh * FOX_AUG_DIM)
    k_aug = jnp.concatenate([k.reshape(t, nh, FOX_HEAD_DIM), ones, -cterms, pad],
                            axis=-1).reshape(bsz, s, nh * FOX_AUG_DIM)
    ot = fox_attention(q_aug, k_aug, vt)
    return proj_residual_ln(ot.T, w_o.astype(BF16), x_f32, ln_g, ln_b, name="fox_out_ln")


def retention_layer(x_f32, x_bf16, bsz, s, w_in, gn_g, w_o, ln_g, ln_b):
    t, d = x_f32.shape
    nh = d // RET_QK_DIM
    qkw = 2 * nh * RET_QK_DIM
    wb = w_in.astype(BF16)
    qk = matmul(x_bf16, wb[:, :qkw], F32, name="ret_qk")
    vg = matmul(x_bf16, wb[:, qkw:], BF16, name="ret_vg")
    half = RET_QK_DIM // 2
    inv_freq = ROPE_BASE ** (-jnp.arange(half, dtype=F32) / half)
    ang = jnp.arange(s, dtype=F32)[:, None] * inv_freq[None, :]
    o = retention(qk.reshape(bsz, s, -1), vg.reshape(bsz, s, -1), jnp.cos(ang), jnp.sin(ang), gn_g)
    return proj_residual_ln(o.reshape(t, -1), w_o.astype(BF16), x_f32, ln_g, ln_b, name="ret_out_ln")


def kernel(x, l0_fox_w_in, l0_fox_b_f, l0_fox_w_o, l0_ln1_g, l0_ln1_b, l0_peer_wq, l0_peer_k1, l0_peer_k2, l0_peer_u, l0_peer_v, l0_ln2_g, l0_ln2_b, l1_ret_w_in, l1_ret_gn_g, l1_ret_w_o, l1_ln1_g, l1_ln1_b, l1_peer_wq, l1_peer_k1, l1_peer_k2, l1_peer_u, l1_peer_v, l1_ln2_g, l1_ln2_b):
    bsz, s, d = x.shape
    t = bsz * s
    xf = x.reshape(t, d).astype(F32)
    xb = xf.astype(BF16)

    xf, xb = fox_layer(xf, xb, bsz, s, l0_fox_w_in, l0_fox_b_f, l0_fox_w_o, l0_ln1_g, l0_ln1_b)
    y = peer_ffn(xf, xb, l0_peer_wq, l0_peer_k1, l0_peer_k2, l0_peer_u, l0_peer_v)
    xf, xb = residual_ln(y, xf, l0_ln2_g, l0_ln2_b, name="peer0_ln")

    xf, xb = retention_layer(xf, xb, bsz, s, l1_ret_w_in, l1_ret_gn_g, l1_ret_w_o, l1_ln1_g, l1_ln1_b)
    y = peer_ffn(xf, xb, l1_peer_wq, l1_peer_k1, l1_peer_k2, l1_peer_u, l1_peer_v)
    xf, _ = residual_ln(y, xf, l1_ln2_g, l1_ln2_b, name="peer1_ln")
    return xf.reshape(bsz, s, d).astype(x.dtype)
```

```python
import functools
import math

import jax
import jax.numpy as jnp
from jax import lax
from jax.experimental import pallas as pl
from jax.experimental.pallas import tpu as pltpu

DEPTH = 2
CHUNK = 64
FOX_HEAD_DIM = 128
RET_QK_DIM = 256
RET_V_DIM = 512
ROPE_BASE = 10000.0
PEER_N_KEYS = 128
PEER_QUERY_DIM = 256
PEER_HALF = PEER_QUERY_DIM // 2
PEER_TOPK = 16
DN_ALPHA = (2 * DEPTH) ** 0.25
LN_EPS = 1e-5

V7X_LANES = 128
V7X_SUBLANES = 8
V7X_VMEM_BYTES = 64 * 1024 * 1024
VMEM_LIMIT = V7X_VMEM_BYTES - 6 * 1024 * 1024

F32 = jnp.float32
BF16 = jnp.bfloat16
NEG_INF = float("-inf")


def _params(sem, vmem=None):
    return pltpu.CompilerParams(dimension_semantics=sem, vmem_limit_bytes=vmem or VMEM_LIMIT)


def _mm_kernel(a_ref, b_ref, o_ref, *, scale):
    acc = jnp.dot(a_ref[...], b_ref[...], preferred_element_type=F32)
    if scale is not None:
        acc = acc * scale
    o_ref[...] = acc.astype(o_ref.dtype)


def matmul(a, b, out_dtype, *, scale=None, tm=1024, tn=1024, name="mm"):
    m, k = a.shape
    _, n = b.shape
    tm = min(tm, m)
    tn = min(tn, n)
    return pl.pallas_call(
        functools.partial(_mm_kernel, scale=scale),
        grid=(n // tn, m // tm),
        in_specs=[pl.BlockSpec((tm, k), lambda j, i: (i, 0)),
                  pl.BlockSpec((k, tn), lambda j, i: (0, j))],
        out_specs=pl.BlockSpec((tm, tn), lambda j, i: (i, j)),
        out_shape=jax.ShapeDtypeStruct((m, n), out_dtype),
        compiler_params=_params(("parallel", "parallel")),
        name=name,
    )(a, b)


def _layer_norm_rows(y, g, b):
    mu = jnp.mean(y, axis=-1, keepdims=True)
    yc = y - mu
    var = jnp.mean(yc * yc, axis=-1, keepdims=True)
    return yc * lax.rsqrt(var + LN_EPS) * g + b


def _proj_ln_kernel(a_ref, w_ref, x_ref, g_ref, b_ref, of_ref, ob_ref):
    y = jnp.dot(a_ref[...], w_ref[...], preferred_element_type=F32) + DN_ALPHA * x_ref[...]
    yn = _layer_norm_rows(y, g_ref[...], b_ref[...])
    of_ref[...] = yn
    ob_ref[...] = yn.astype(BF16)


def proj_residual_ln(a, w, x, g, b, *, tm=256, name="proj_ln"):
    m, k = a.shape
    d = w.shape[1]
    tm = min(tm, m)
    return pl.pallas_call(
        _proj_ln_kernel,
        grid=(m // tm,),
        in_specs=[pl.BlockSpec((tm, k), lambda i: (i, 0)),
                  pl.BlockSpec((k, d), lambda i: (0, 0)),
                  pl.BlockSpec((tm, d), lambda i: (i, 0)),
                  pl.BlockSpec((1, d), lambda i: (0, 0)),
                  pl.BlockSpec((1, d), lambda i: (0, 0))],
        out_specs=[pl.BlockSpec((tm, d), lambda i: (i, 0)),
                   pl.BlockSpec((tm, d), lambda i: (i, 0))],
        out_shape=[jax.ShapeDtypeStruct((m, d), F32), jax.ShapeDtypeStruct((m, d), BF16)],
        compiler_params=_params(("parallel",)),
        name=name,
    )(a, w, x, g.reshape(1, d), b.reshape(1, d))


def _residual_ln_kernel(y_ref, x_ref, g_ref, b_ref, of_ref, ob_ref):
    y = y_ref[...] + DN_ALPHA * x_ref[...]
    yn = _layer_norm_rows(y, g_ref[...], b_ref[...])
    of_ref[...] = yn
    ob_ref[...] = yn.astype(BF16)


def residual_ln(y, x, g, b, *, tm=512, name="res_ln"):
    m, d = x.shape
    tm = min(tm, m)
    return pl.pallas_call(
        _residual_ln_kernel,
        grid=(m // tm,),
        in_specs=[pl.BlockSpec((tm, d), lambda i: (i, 0)),
                  pl.BlockSpec((tm, d), lambda i: (i, 0)),
                  pl.BlockSpec((1, d), lambda i: (0, 0)),
                  pl.BlockSpec((1, d), lambda i: (0, 0))],
        out_specs=[pl.BlockSpec((tm, d), lambda i: (i, 0)),
                   pl.BlockSpec((tm, d), lambda i: (i, 0))],
        out_shape=[jax.ShapeDtypeStruct((m, d), F32), jax.ShapeDtypeStruct((m, d), BF16)],
        compiler_params=_params(("parallel",)),
        name=name,
    )(y, x, g.reshape(1, d), b.reshape(1, d))


def _forget_cumsum_kernel(g_ref, bf_ref, c_ref, carry_ref):
    @pl.when(pl.program_id(1) == 0)
    def _():
        carry_ref[...] = jnp.zeros_like(carry_ref)

    z = g_ref[0] + bf_ref[...]
    lf = jnp.minimum(z, 0.0) - jnp.log1p(jnp.exp(-jnp.abs(z)))
    tc = lf.shape[0]
    row = lax.broadcasted_iota(jnp.int32, (tc, tc), 0)
    col = lax.broadcasted_iota(jnp.int32, (tc, tc), 1)
    tri = jnp.where(row >= col, 1.0, 0.0).astype(BF16)
    hi = lf.astype(BF16)
    lo = (lf - hi.astype(F32)).astype(BF16)
    cs = (jnp.dot(tri, hi, preferred_element_type=F32)
          + jnp.dot(tri, lo, preferred_element_type=F32))
    c = cs + carry_ref[...]
    c_ref[0] = c
    carry_ref[...] = c[tc - 1:tc, :]


def forget_cumsum(gate_logits, b_f_row, *, tc=512):
    bsz, s, w = gate_logits.shape
    tc = min(tc, s)
    return pl.pallas_call(
        _forget_cumsum_kernel,
        grid=(bsz, s // tc),
        in_specs=[pl.BlockSpec((1, tc, w), lambda b, i: (b, i, 0)),
                  pl.BlockSpec((1, w), lambda b, i: (0, 0))],
        out_specs=pl.BlockSpec((1, tc, w), lambda b, i: (b, i, 0)),
        out_shape=jax.ShapeDtypeStruct((bsz, s, w), F32),
        scratch_shapes=[pltpu.VMEM((1, w), F32)],
        compiler_params=_params(("parallel", "arbitrary")),
        name="forget_cumsum",
    )(gate_logits, b_f_row)


LOG2E = math.log2(math.e)


FOX_HEADS_PER_STEP = 2


FOX_AUG_DIM = 2 * FOX_HEAD_DIM
FOX_BIAS_TERMS = 3


def _fox_kernel(qi_ref, ki_ref, q_ref, qc_ref, k_ref, kc_ref, rq_ref, rk_ref, vt_ref, o_ref,
                m_ref, l_ref, acc_ref, *, tb):
    step_id = pl.program_id(2)
    qi = qi_ref[step_id]
    ki = ki_ref[step_id]
    hd = FOX_HEAD_DIM

    @pl.when(ki == 0)
    def _():
        m_ref[...] = jnp.full_like(m_ref, NEG_INF)
        l_ref[...] = jnp.zeros_like(l_ref)
        acc_ref[...] = jnp.zeros_like(acc_ref)

    def step(masked, last):
        for g in range(FOX_HEADS_PER_STEP):
            rows = slice(g * hd, (g + 1) * hd)
            k_aug = jnp.concatenate([k_ref[0, :, rows], kc_ref[0, :, rows]], axis=1)
            q_aug = jnp.concatenate([q_ref[0, :, rows], qc_ref[0, :, rows]], axis=1)
            st = lax.dot_general(k_aug, q_aug, (((1,), (1,)), ((), ())),
                                 preferred_element_type=F32)
            if masked:
                key = lax.broadcasted_iota(jnp.int32, (tb, tb), 0)
                qry = lax.broadcasted_iota(jnp.int32, (tb, tb), 1)
                st = jnp.where(key <= qry, st, NEG_INF)
            delta = rq_ref[0, 0, 0, g:g + 1, :] - rk_ref[0, 0, 0, g:g + 1, :]
            m_prev = m_ref[g]
            m_new = jnp.maximum(m_prev, jnp.max(st, axis=0, keepdims=True) + delta)
            alpha = jnp.exp2(m_prev - m_new)
            p = jnp.exp2(st + (delta - m_new))
            l_new = alpha * l_ref[g] + jnp.sum(p, axis=0, keepdims=True)
            acc_new = alpha * acc_ref[g] + jnp.dot(vt_ref[rows, :], p.astype(BF16),
                                                   preferred_element_type=F32)
            if last:
                o_ref[rows, :] = (acc_new / l_new).astype(o_ref.dtype)
            else:
                m_ref[g] = m_new
                l_ref[g] = l_new
                acc_ref[g] = acc_new

    @pl.when(ki < qi)
    def _():
        step(False, False)

    @pl.when(ki == qi)
    def _():
        step(True, True)


FOX_BLOCK = 512


def fox_attention(q, qc, k, kc, ref, vt):
    bsz, s, d = q.shape
    nh = d // FOX_HEAD_DIM
    hps = FOX_HEADS_PER_STEP
    assert nh % hps == 0
    tb = min(FOX_BLOCK, s)
    nb = s // tb
    pairs = [(qi, ki) for qi in range(nb) for ki in range(qi + 1)]
    qi_arr = jnp.asarray([p[0] for p in pairs], jnp.int32)
    ki_arr = jnp.asarray([p[1] for p in pairs], jnp.int32)
    wide = hps * FOX_HEAD_DIM
    grid_spec = pltpu.PrefetchScalarGridSpec(
        num_scalar_prefetch=2,
        grid=(bsz, nh // hps, len(pairs)),
        in_specs=[
            pl.BlockSpec((1, tb, wide), lambda b, h, i, qa, ka: (b, qa[i], h)),
            pl.BlockSpec((1, tb, wide), lambda b, h, i, qa, ka: (b, qa[i], h)),
            pl.BlockSpec((1, tb, wide), lambda b, h, i, qa, ka: (b, ka[i], h)),
            pl.BlockSpec((1, tb, wide), lambda b, h, i, qa, ka: (b, ka[i], h)),
            pl.BlockSpec((1, 1, 1, hps, tb), lambda b, h, i, qa, ka: (b, h, qa[i], 0, 0)),
            pl.BlockSpec((1, 1, 1, hps, tb), lambda b, h, i, qa, ka: (b, h, ka[i], 0, 0)),
            pl.BlockSpec((hps * FOX_HEAD_DIM, tb), lambda b, h, i, qa, ka: (h, b * nb + ka[i])),
        ],
        out_specs=pl.BlockSpec((hps * FOX_HEAD_DIM, tb), lambda b, h, i, qa, ka: (h, b * nb + qa[i])),
        scratch_shapes=[pltpu.VMEM((hps, 1, tb), F32), pltpu.VMEM((hps, 1, tb), F32),
                        pltpu.VMEM((hps, FOX_HEAD_DIM, tb), F32)],
    )
    return pl.pallas_call(
        functools.partial(_fox_kernel, tb=tb),
        grid_spec=grid_spec,
        out_shape=jax.ShapeDtypeStruct(vt.shape, BF16),
        compiler_params=_params(("parallel", "parallel", "arbitrary")),
        name="fox_attention",
    )(qi_arr, ki_arr, q, qc, k, kc, ref, ref, vt)


def _rope_rows(a, cos, sin):
    half = a.shape[-1] // 2
    a1, a2 = a[:, :half], a[:, half:]
    return jnp.concatenate([a1 * cos - a2 * sin, a1 * sin + a2 * cos], axis=-1)


def _retention_kernel(q_ref, k_ref, v_ref, gate_ref, cos_ref, sin_ref, dmat_ref, qd_ref, kd_ref,
                      bd_ref, gn_ref, o_ref, state_ref):
    @pl.when(pl.program_id(2) == 0)
    def _():
        state_ref[...] = jnp.zeros_like(state_ref)

    cos = cos_ref[...]
    sin = sin_ref[...]
    q = _rope_rows(q_ref[0], cos, sin)
    k = _rope_rows(k_ref[0], cos, sin) * (RET_QK_DIM ** -0.5)
    v = v_ref[0]
    scores = lax.dot_general(q.astype(BF16), k.astype(BF16), (((1,), (1,)), ((), ())),
                             preferred_element_type=F32) * dmat_ref[0]
    out = jnp.dot(scores.astype(BF16), v, preferred_element_type=F32)
    state = state_ref[...]
    out = out + jnp.dot((q * qd_ref[0]).astype(BF16), state.astype(BF16),
                        preferred_element_type=F32)
    kdt = jnp.transpose(k * kd_ref[0]).astype(BF16)
    state_ref[...] = state * bd_ref[0] + jnp.dot(kdt, v, preferred_element_type=F32)
    mu = jnp.mean(out, axis=-1, keepdims=True)
    oc = out - mu
    var = jnp.mean(oc * oc, axis=-1, keepdims=True)
    o = oc * lax.rsqrt(var + LN_EPS) * gn_ref[...]
    g = gate_ref[0].astype(F32)
    o_ref[0] = (g * jax.nn.sigmoid(g) * o).astype(o_ref.dtype)


def _retention_tables(nh, blk):
    log_gamma = jnp.log(1.0 - jnp.exp2(-5.0 - jnp.arange(nh, dtype=F32)))
    t = jnp.arange(blk, dtype=F32)
    chunk_id = jnp.arange(blk) // CHUNK
    dist = jnp.abs(t[:, None] - t[None, :])
    allowed = chunk_id[None, :] <= chunk_id[:, None]
    dmat = jnp.where(allowed[None], jnp.exp(log_gamma[:, None, None] * dist[None]), 0.0)
    qd = jnp.exp(log_gamma[:, None] * (t[None, :] + 1.0))[..., None]
    kd = jnp.exp(log_gamma[:, None] * (blk - 1.0 - t[None, :]))[..., None]
    bd = jnp.exp(log_gamma * blk).reshape(nh, 1, 1)
    return dmat.astype(F32), qd, kd, bd


def retention(qk, vg, cos, sin, gn_g, *, blk=256):
    bsz, s, w = qk.shape
    nh = w // (2 * RET_QK_DIM)
    blk = min(blk, s)
    dmat, qd, kd, bd = _retention_tables(nh, blk)
    half = RET_QK_DIM // 2
    return pl.pallas_call(
        _retention_kernel,
        grid=(bsz, nh, s // blk),
        in_specs=[
            pl.BlockSpec((1, blk, RET_QK_DIM), lambda b, h, i: (b, i, h)),
            pl.BlockSpec((1, blk, RET_QK_DIM), lambda b, h, i: (b, i, nh + h)),
            pl.BlockSpec((1, blk, RET_V_DIM), lambda b, h, i: (b, i, h)),
            pl.BlockSpec((1, blk, RET_V_DIM), lambda b, h, i: (b, i, nh + h)),
            pl.BlockSpec((blk, half), lambda b, h, i: (i, 0)),
            pl.BlockSpec((blk, half), lambda b, h, i: (i, 0)),
            pl.BlockSpec((1, blk, blk), lambda b, h, i: (h, 0, 0)),
            pl.BlockSpec((1, blk, 1), lambda b, h, i: (h, 0, 0)),
            pl.BlockSpec((1, blk, 1), lambda b, h, i: (h, 0, 0)),
            pl.BlockSpec((1, 1, 1), lambda b, h, i: (h, 0, 0)),
            pl.BlockSpec((1, RET_V_DIM), lambda b, h, i: (0, h)),
        ],
        out_specs=pl.BlockSpec((1, blk, RET_V_DIM), lambda b, h, i: (b, i, h)),
        out_shape=jax.ShapeDtypeStruct((bsz, s, nh * RET_V_DIM), BF16),
        scratch_shapes=[pltpu.VMEM((RET_QK_DIM, RET_V_DIM), F32)],
        compiler_params=_params(("parallel", "parallel", "arbitrary")),
        name="retention",
    )(qk, qk, vg, vg, cos, sin, dmat, qd, kd, bd, gn_g.reshape(1, -1))


def _peer_scores_kernel(x_ref, wq_ref, k1_ref, k2_ref, s1_ref, s2_ref):
    q = jnp.dot(x_ref[...], wq_ref[...], preferred_element_type=F32).astype(BF16)
    nh = s1_ref.shape[0]
    nt = (((1,), (1,)), ((), ()))
    for h in range(nh):
        lo = h * PEER_QUERY_DIM
        s1_ref[h] = lax.dot_general(k1_ref[...], q[:, lo:lo + PEER_HALF], nt,
                                    preferred_element_type=F32)
        s2_ref[h] = lax.dot_general(k2_ref[...], q[:, lo + PEER_HALF:lo + PEER_QUERY_DIM], nt,
                                    preferred_element_type=F32)


def peer_scores(xb, wq, k1, k2, *, tm=256):
    t, d = xb.shape
    nh = wq.shape[1] // PEER_QUERY_DIM
    tm = min(tm, t)
    shp = jax.ShapeDtypeStruct((nh, PEER_N_KEYS, t), F32)
    return pl.pallas_call(
        _peer_scores_kernel,
        grid=(t // tm,),
        in_specs=[pl.BlockSpec((tm, d), lambda i: (i, 0)),
                  pl.BlockSpec(wq.shape, lambda i: (0, 0)),
                  pl.BlockSpec(k1.shape, lambda i: (0, 0)),
                  pl.BlockSpec(k2.shape, lambda i: (0, 0))],
        out_specs=[pl.BlockSpec((nh, PEER_N_KEYS, tm), lambda i: (0, 0, i)),
                   pl.BlockSpec((nh, PEER_N_KEYS, tm), lambda i: (0, 0, i))],
        out_shape=[shp, shp],
        compiler_params=_params(("parallel",)),
        name="peer_scores",
    )(xb, wq, k1, k2)


_PEER_CANDS = tuple((i, j) for i in range(PEER_TOPK) for j in range(PEER_TOPK)
                    if (i + 1) * (j + 1) <= PEER_TOPK)


def _top16_keys(s):
    nh, _, tt = s.shape
    key = lax.broadcasted_iota(jnp.int32, s.shape, 1)
    sub = lax.broadcasted_iota(jnp.int32, (nh, tt), 0)

    def pack_heads(a):
        out = jnp.broadcast_to(a[0], (nh, tt))
        for h in range(1, nh):
            out = jnp.where(sub == h, jnp.broadcast_to(a[h], (nh, tt)), out)
        return out

    vals, idxs = [], []
    for _ in range(PEER_TOPK):
        m = jnp.max(s, axis=1, keepdims=True)
        idx = jnp.min(jnp.where(s == m, key, PEER_N_KEYS), axis=1, keepdims=True)
        s = jnp.where(key == idx, NEG_INF, s)
        vals.append(pack_heads(m))
        idxs.append(pack_heads(idx))
    return vals, idxs


def _peer_topk_kernel(s1_ref, s2_ref, eid_ref, g_ref):
    v1, i1 = _top16_keys(s1_ref[...])
    v2, i2 = _top16_keys(s2_ref[...])
    cand = [v1[i] + v2[j] for (i, j) in _PEER_CANDS]
    ceid = [i1[i] * PEER_N_KEYS + i2[j] for (i, j) in _PEER_CANDS]
    ncand = len(_PEER_CANDS)
    scores, eids = [], []
    for _ in range(PEER_TOPK):
        m = functools.reduce(jnp.maximum, cand)
        pos = functools.reduce(jnp.minimum,
                               [jnp.where(cand[c] == m, c, ncand) for c in range(ncand)])
        e = jnp.zeros_like(ceid[0])
        for c in range(ncand):
            hit = pos == c
            e = jnp.where(hit, ceid[c], e)
            cand[c] = jnp.where(hit, NEG_INF, cand[c])
        scores.append(m)
        eids.append(e)
    ex = [jnp.exp(sc - scores[0]) for sc in scores]
    denom = functools.reduce(jnp.add, ex)
    eid_ref[...] = jnp.transpose(jnp.concatenate(eids, axis=0))
    g_ref[...] = jnp.transpose(jnp.concatenate([e / denom for e in ex], axis=0))


def peer_topk(s1, s2, *, tt=128):
    nh, nk, t = s1.shape
    tt = min(tt, t)
    return pl.pallas_call(
        _peer_topk_kernel,
        grid=(t // tt,),
        in_specs=[pl.BlockSpec((nh, nk, tt), lambda i: (0, 0, i)),
                  pl.BlockSpec((nh, nk, tt), lambda i: (0, 0, i))],
        out_specs=[pl.BlockSpec((tt, PEER_TOPK * nh), lambda i: (i, 0)),
                   pl.BlockSpec((tt, PEER_TOPK * nh), lambda i: (i, 0))],
        out_shape=[jax.ShapeDtypeStruct((t, PEER_TOPK * nh), jnp.int32),
                   jax.ShapeDtypeStruct((t, PEER_TOPK * nh), F32)],
        compiler_params=_params(("parallel",)),
        name="peer_topk",
    )(s1, s2)


PEER_GROUP = 16


def _split_bf16(a):
    hi = a.astype(BF16)
    lo = (a - hi.astype(F32)).astype(BF16)
    return jnp.concatenate([hi, lo], axis=0)


def _diag_mask():
    sub = lax.broadcasted_iota(jnp.int32, (V7X_SUBLANES, V7X_LANES), 0)
    lane = lax.broadcasted_iota(jnp.int32, (V7X_SUBLANES, V7X_LANES), 1)
    return (lane & (V7X_SUBLANES - 1)) == sub


PEER_TOKENS_PER_STEP = 32


def _gather_tiles(tab_ref, eid_ref, first):
    tiles = []
    for lo in range(0, PEER_GROUP, V7X_SUBLANES):
        part = eid_ref.at[pl.ds(first + lo, V7X_SUBLANES)]
        tiles += [tab_ref[part[j]] for j in range(V7X_SUBLANES)]
    return jnp.concatenate(tiles, axis=0)


def _for_each_token(tb, npairs, body):
    def trip(i, carry):
        for u in range(PEER_TOKENS_PER_STEP):
            t = i * PEER_TOKENS_PER_STEP + u
            body(t, t * npairs)
        return carry

    lax.fori_loop(0, tb // PEER_TOKENS_PER_STEP, trip, 0)


def _peer_up_kernel(eid_ref, x_ref, tab_ref, o_ref, zs_ref, *, npairs):
    tb = x_ref.shape[0]
    ngroups = npairs // PEER_GROUP
    diag = _diag_mask()
    nt = (((1,), (1,)), ((), ()))

    sub = lax.broadcasted_iota(jnp.int32, (V7X_SUBLANES, V7X_LANES), 0)

    octets = PEER_TOKENS_PER_STEP // V7X_SUBLANES

    def trip(i, carry):
        for o in range(octets):
            tiles = [jnp.zeros((V7X_SUBLANES, V7X_LANES), F32) for _ in range(ngroups)]
            for u in range(V7X_SUBLANES):
                t = (i * octets + o) * V7X_SUBLANES + u
                lhs = _split_bf16(x_ref[t])
                for k in range(ngroups):
                    g = _gather_tiles(tab_ref, eid_ref, t * npairs + k * PEER_GROUP)
                    z = lax.dot_general(lhs, g, nt, preferred_element_type=F32)
                    zd = jnp.where(diag, z[:V7X_SUBLANES] + z[V7X_SUBLANES:], 0.0)
                    for shift in (1, 2, 4):
                        zd = zd + pltpu.roll(zd, shift=shift, axis=0)
                    tiles[k] = jnp.where(sub == u, zd, tiles[k])
            for k in range(ngroups):
                zs_ref[k, i * octets + o] = tiles[k]
        return carry

    lax.fori_loop(0, tb // PEER_TOKENS_PER_STEP, trip, 0)

    row = lax.broadcasted_iota(jnp.int32, (V7X_LANES, npairs), 0)
    col = lax.broadcasted_iota(jnp.int32, (V7X_LANES, npairs), 1)
    acc = jnp.zeros((tb, npairs), F32)
    for k in range(ngroups):
        zr = zs_ref[k].reshape(tb, V7X_LANES)
        sel = jnp.where(col == k * PEER_GROUP + row // V7X_SUBLANES, 1.0, 0.0).astype(BF16)
        zh = zr.astype(BF16)
        zl = (zr - zh.astype(F32)).astype(BF16)
        acc = acc + jnp.dot(zh, sel, preferred_element_type=F32) + jnp.dot(zl, sel, preferred_element_type=F32)
    o_ref[...] = acc


def peer_up(eid_tok, x4, utab, *, tb=128):
    t, npairs = eid_tok.shape
    nexp, nhalf, r, lanes = utab.shape
    tb = min(tb, t)
    assert r == V7X_SUBLANES and lanes == V7X_LANES and npairs % PEER_GROUP == 0
    return pl.pallas_call(
        functools.partial(_peer_up_kernel, npairs=npairs),
        grid=(nhalf, t // tb),
        in_specs=[pl.BlockSpec((tb * npairs,), lambda c, i: (i,), memory_space=pltpu.SMEM),
                  pl.BlockSpec((tb, None, r, lanes), lambda c, i: (i, c, 0, 0)),
                  pl.BlockSpec((nexp, None, r, lanes), lambda c, i: (0, c, 0, 0),
                               pipeline_mode=pl.Buffered(1))],
        out_specs=pl.BlockSpec((None, tb, npairs), lambda c, i: (c, i, 0)),
        out_shape=jax.ShapeDtypeStruct((nhalf, t, npairs), F32),
        scratch_shapes=[pltpu.VMEM((npairs // PEER_GROUP, tb // V7X_SUBLANES, r, lanes), F32)],
        compiler_params=_params(("arbitrary", "arbitrary")),
        name="peer_up",
    )(eid_tok.reshape(-1), x4, utab)


def _peer_act_kernel(h_ref, g_ref, w_ref):
    h = h_ref[0] + h_ref[1]
    gelu = 0.5 * h * (1.0 + lax.erf(h * (2.0 ** -0.5)))
    w_ref[...] = g_ref[...] * gelu


def peer_act(h2, g, *, tt=512):
    _, t, npairs = h2.shape
    tt = min(tt, t)
    return pl.pallas_call(
        _peer_act_kernel,
        grid=(t // tt,),
        in_specs=[pl.BlockSpec((2, tt, npairs), lambda i: (0, i, 0)),
                  pl.BlockSpec((tt, npairs), lambda i: (i, 0))],
        out_specs=pl.BlockSpec((tt, npairs), lambda i: (i, 0)),
        out_shape=jax.ShapeDtypeStruct((t, npairs), F32),
        compiler_params=_params(("parallel",)),
        name="peer_act",
    )(h2, g)


def _peer_down_kernel(eid_ref, w_ref, tab_ref, o_ref, *, npairs):
    tb = o_ref.shape[0]
    ngroups = npairs // PEER_GROUP
    diag = _diag_mask()

    def token(t, first):
        wall = w_ref[t]
        acc = jnp.zeros((2 * V7X_SUBLANES, V7X_LANES), F32)
        for k in range(ngroups):
            g = _gather_tiles(tab_ref, eid_ref, first + k * PEER_GROUP)
            wk = jnp.broadcast_to(wall[k:k + 1, :], (V7X_SUBLANES, V7X_LANES))
            lhs = _split_bf16(jnp.where(diag, wk, 0.0))
            acc = acc + jnp.dot(lhs, g, preferred_element_type=F32)
        o_ref[t] = acc[:V7X_SUBLANES] + acc[V7X_SUBLANES:]

    _for_each_token(tb, npairs, token)


def peer_down(eid_tok, w_rep, vtab, *, tb=128):
    t, npairs = eid_tok.shape
    nexp, nhalf, r, lanes = vtab.shape
    tb = min(tb, t)
    assert r == V7X_SUBLANES and lanes == V7X_LANES and npairs // PEER_GROUP == V7X_SUBLANES
    return pl.pallas_call(
        functools.partial(_peer_down_kernel, npairs=npairs),
        grid=(nhalf, t // tb),
        in_specs=[pl.BlockSpec((tb * npairs,), lambda c, i: (i,), memory_space=pltpu.SMEM),
                  pl.BlockSpec((tb, npairs // PEER_GROUP, lanes), lambda c, i: (i, 0, 0)),
                  pl.BlockSpec((nexp, None, r, lanes), lambda c, i: (0, c, 0, 0),
                               pipeline_mode=pl.Buffered(1))],
        out_specs=pl.BlockSpec((tb, None, r, lanes), lambda c, i: (i, c, 0, 0)),
        out_shape=jax.ShapeDtypeStruct((t, nhalf, r, lanes), F32),
        compiler_params=_params(("arbitrary", "arbitrary")),
        name="peer_down",
    )(eid_tok.reshape(-1), w_rep, vtab)


def _expert_table(w):
    nexp, d = w.shape
    r = d // (2 * V7X_LANES)
    return w.astype(BF16).reshape(nexp, 2, r, V7X_LANES)


def peer_ffn(x_f32, x_bf16, wq, k1, k2, u, v):
    t, d = x_f32.shape
    r = d // (2 * V7X_LANES)
    s1, s2 = peer_scores(x_bf16, wq.astype(BF16), k1.astype(BF16), k2.astype(BF16))
    eid_tok, gate = peer_topk(s1, s2)
    npairs = eid_tok.shape[1]
    h2 = peer_up(eid_tok, x_f32.reshape(t, 2, r, V7X_LANES), _expert_table(u))
    w = peer_act(h2, gate)
    w_rep = jnp.repeat(w, V7X_SUBLANES, axis=1).reshape(t, npairs // PEER_GROUP, V7X_LANES)
    y = peer_down(eid_tok, w_rep, _expert_table(v))
    return y.reshape(t, d)


def fox_layer(x_f32, x_bf16, bsz, s, w_in, b_f, w_o, ln_g, ln_b):
    t, d = x_f32.shape
    nh = d // FOX_HEAD_DIM
    wb = w_in.astype(BF16)
    q2 = matmul(x_bf16, wb[:, :d], BF16, scale=LOG2E * FOX_HEAD_DIM ** -0.5, name="fox_q")
    k = matmul(x_bf16, wb[:, d:2 * d], BF16, name="fox_k")
    vt = matmul(wb[:, 2 * d:3 * d].T, x_bf16.T, BF16, name="fox_vt")
    wg = jnp.pad(wb[:, 3 * d:], ((0, 0), (0, V7X_LANES - nh)))
    gl = matmul(x_bf16, wg, F32, tn=V7X_LANES, name="fox_gate")
    bf_row = jnp.pad(b_f.astype(F32), (0, V7X_LANES - nh)).reshape(1, V7X_LANES)
    c = forget_cumsum(gl.reshape(bsz, s, V7X_LANES), bf_row)[:, :, :nh]
    tb = min(FOX_BLOCK, s)
    nb = s // tb
    hps = FOX_HEADS_PER_STEP
    c2 = c * LOG2E
    ref = c2[:, ::tb, :]
    rel = (c2.reshape(bsz, nb, tb, nh) - ref[:, :, None, :]).reshape(t, nh)
    terms, rest = [], rel
    for _ in range(FOX_BIAS_TERMS):
        part = rest.astype(BF16)
        terms.append(part)
        rest = rest - part.astype(F32)
    cterms = jnp.stack(terms, axis=-1)
    ones = jnp.ones_like(cterms)
    pad = jnp.zeros((t, nh, FOX_HEAD_DIM - 2 * FOX_BIAS_TERMS), BF16)
    qc = jnp.concatenate([cterms, ones, pad], axis=-1).reshape(bsz, s, d)
    kc = jnp.concatenate([ones, -cterms, pad], axis=-1).reshape(bsz, s, d)
    ref_rows = jnp.broadcast_to(
        jnp.transpose(ref, (0, 2, 1)).reshape(bsz, nh // hps, hps, nb).transpose(0, 1, 3, 2)[..., None],
        (bsz, nh // hps, nb, hps, tb))
    ot = fox_attention(q2.reshape(bsz, s, d), qc, k.reshape(bsz, s, d), kc, ref_rows, vt)
    return proj_residual_ln(ot.T, w_o.astype(BF16), x_f32, ln_g, ln_b, name="fox_out_ln")


def retention_layer(x_f32, x_bf16, bsz, s, w_in, gn_g, w_o, ln_g, ln_b):
    t, d = x_f32.shape
    nh = d // RET_QK_DIM
    qkw = 2 * nh * RET_QK_DIM
    wb = w_in.astype(BF16)
    qk = matmul(x_bf16, wb[:, :qkw], F32, name="ret_qk")
    vg = matmul(x_bf16, wb[:, qkw:], BF16, name="ret_vg")
    half = RET_QK_DIM // 2
    inv_freq = ROPE_BASE ** (-jnp.arange(half, dtype=F32) / half)
    ang = jnp.arange(s, dtype=F32)[:, None] * inv_freq[None, :]
    o = retention(qk.reshape(bsz, s, -1), vg.reshape(bsz, s, -1), jnp.cos(ang), jnp.sin(ang), gn_g)
    return proj_residual_ln(o.reshape(t, -1), w_o.astype(BF16), x_f32, ln_g, ln_b, name="ret_out_ln")


def kernel(x, l0_fox_w_in, l0_fox_b_f, l0_fox_w_o, l0_ln1_g, l0_ln1_b, l0_peer_wq, l0_peer_k1, l0_peer_k2, l0_peer_u, l0_peer_v, l0_ln2_g, l0_ln2_b, l1_ret_w_in, l1_ret_gn_g, l1_ret_w_o, l1_ln1_g, l1_ln1_b, l1_peer_wq, l1_peer_k1, l1_peer_k2, l1_peer_u, l1_peer_v, l1_ln2_g, l1_ln2_b):
    bsz, s, d = x.shape
    t = bsz * s
    xf = x.reshape(t, d).astype(F32)
    xb = xf.astype(BF16)

    xf, xb = fox_layer(xf, xb, bsz, s, l0_fox_w_in, l0_fox_b_f, l0_fox_w_o, l0_ln1_g, l0_ln1_b)
    y = peer_ffn(xf, xb, l0_peer_wq, l0_peer_k1, l0_peer_k2, l0_peer_u, l0_peer_v)
    xf, xb = residual_ln(y, xf, l0_ln2_g, l0_ln2_b, name="peer0_ln")

    xf, xb = retention_layer(xf, xb, bsz, s, l1_ret_w_in, l1_ret_gn_g, l1_ret_w_o, l1_ln1_g, l1_ln1_b)
    y = peer_ffn(xf, xb, l1_peer_wq, l1_peer_k1, l1_peer_k2, l1_peer_u, l1_peer_v)
    xf, _ = residual_ln(y, xf, l1_ln2_g, l1_ln2_b, name="peer1_ln")
    return xf.reshape(bsz, s, d).astype(x.dtype)
```

```python
import functools
import math

import jax
import jax.numpy as jnp
from jax import lax
from jax.experimental import pallas as pl
from jax.experimental.pallas import tpu as pltpu

DEPTH = 2
CHUNK = 64
FOX_HEAD_DIM = 128
RET_QK_DIM = 256
RET_V_DIM = 512
ROPE_BASE = 10000.0
PEER_N_KEYS = 128
PEER_QUERY_DIM = 256
PEER_HALF = PEER_QUERY_DIM // 2
PEER_TOPK = 16
DN_ALPHA = (2 * DEPTH) ** 0.25
LN_EPS = 1e-5

V7X_LANES = 128
V7X_SUBLANES = 8
V7X_VMEM_BYTES = 64 * 1024 * 1024
VMEM_LIMIT = V7X_VMEM_BYTES - 6 * 1024 * 1024

F32 = jnp.float32
BF16 = jnp.bfloat16
NEG_INF = float("-inf")


def _params(sem, vmem=None):
    return pltpu.CompilerParams(dimension_semantics=sem, vmem_limit_bytes=vmem or VMEM_LIMIT)


def _mm_kernel(a_ref, b_ref, o_ref, *, scale):
    acc = jnp.dot(a_ref[...], b_ref[...], preferred_element_type=F32)
    if scale is not None:
        acc = acc * scale
    o_ref[...] = acc.astype(o_ref.dtype)


def matmul(a, b, out_dtype, *, scale=None, tm=1024, tn=1024, name="mm"):
    m, k = a.shape
    _, n = b.shape
    tm = min(tm, m)
    tn = min(tn, n)
    return pl.pallas_call(
        functools.partial(_mm_kernel, scale=scale),
        grid=(n // tn, m // tm),
        in_specs=[pl.BlockSpec((tm, k), lambda j, i: (i, 0)),
                  pl.BlockSpec((k, tn), lambda j, i: (0, j))],
        out_specs=pl.BlockSpec((tm, tn), lambda j, i: (i, j)),
        out_shape=jax.ShapeDtypeStruct((m, n), out_dtype),
        compiler_params=_params(("parallel", "parallel")),
        name=name,
    )(a, b)


def _layer_norm_rows(y, g, b):
    mu = jnp.mean(y, axis=-1, keepdims=True)
    yc = y - mu
    var = jnp.mean(yc * yc, axis=-1, keepdims=True)
    return yc * lax.rsqrt(var + LN_EPS) * g + b


def _proj_ln_kernel(a_ref, w_ref, x_ref, g_ref, b_ref, of_ref, ob_ref):
    y = jnp.dot(a_ref[...], w_ref[...], preferred_element_type=F32) + DN_ALPHA * x_ref[...]
    yn = _layer_norm_rows(y, g_ref[...], b_ref[...])
    of_ref[...] = yn
    ob_ref[...] = yn.astype(BF16)


def proj_residual_ln(a, w, x, g, b, *, tm=256, name="proj_ln"):
    m, k = a.shape
    d = w.shape[1]
    tm = min(tm, m)
    return pl.pallas_call(
        _proj_ln_kernel,
        grid=(m // tm,),
        in_specs=[pl.BlockSpec((tm, k), lambda i: (i, 0)),
                  pl.BlockSpec((k, d), lambda i: (0, 0)),
                  pl.BlockSpec((tm, d), lambda i: (i, 0)),
                  pl.BlockSpec((1, d), lambda i: (0, 0)),
                  pl.BlockSpec((1, d), lambda i: (0, 0))],
        out_specs=[pl.BlockSpec((tm, d), lambda i: (i, 0)),
                   pl.BlockSpec((tm, d), lambda i: (i, 0))],
        out_shape=[jax.ShapeDtypeStruct((m, d), F32), jax.ShapeDtypeStruct((m, d), BF16)],
        compiler_params=_params(("parallel",)),
        name=name,
    )(a, w, x, g.reshape(1, d), b.reshape(1, d))


def _residual_ln_kernel(y_ref, x_ref, g_ref, b_ref, of_ref, ob_ref):
    y = y_ref[...] + DN_ALPHA * x_ref[...]
    yn = _layer_norm_rows(y, g_ref[...], b_ref[...])
    of_ref[...] = yn
    ob_ref[...] = yn.astype(BF16)


def residual_ln(y, x, g, b, *, tm=512, name="res_ln"):
    m, d = x.shape
    tm = min(tm, m)
    return pl.pallas_call(
        _residual_ln_kernel,
        grid=(m // tm,),
        in_specs=[pl.BlockSpec((tm, d), lambda i: (i, 0)),
                  pl.BlockSpec((tm, d), lambda i: (i, 0)),
                  pl.BlockSpec((1, d), lambda i: (0, 0)),
                  pl.BlockSpec((1, d), lambda i: (0, 0))],
        out_specs=[pl.BlockSpec((tm, d), lambda i: (i, 0)),
                   pl.BlockSpec((tm, d), lambda i: (i, 0))],
        out_shape=[jax.ShapeDtypeStruct((m, d), F32), jax.ShapeDtypeStruct((m, d), BF16)],
        compiler_params=_params(("parallel",)),
        name=name,
    )(y, x, g.reshape(1, d), b.reshape(1, d))


def _forget_cumsum_kernel(g_ref, bf_ref, c_ref, carry_ref):
    @pl.when(pl.program_id(1) == 0)
    def _():
        carry_ref[...] = jnp.zeros_like(carry_ref)

    z = g_ref[0] + bf_ref[...]
    lf = jnp.minimum(z, 0.0) - jnp.log1p(jnp.exp(-jnp.abs(z)))
    tc = lf.shape[0]
    row = lax.broadcasted_iota(jnp.int32, (tc, tc), 0)
    col = lax.broadcasted_iota(jnp.int32, (tc, tc), 1)
    tri = jnp.where(row >= col, 1.0, 0.0).astype(BF16)
    hi = lf.astype(BF16)
    lo = (lf - hi.astype(F32)).astype(BF16)
    cs = (jnp.dot(tri, hi, preferred_element_type=F32)
          + jnp.dot(tri, lo, preferred_element_type=F32))
    c = cs + carry_ref[...]
    c_ref[0] = c
    carry_ref[...] = c[tc - 1:tc, :]


def forget_cumsum(gate_logits, b_f_row, *, tc=512):
    bsz, s, w = gate_logits.shape
    tc = min(tc, s)
    return pl.pallas_call(
        _forget_cumsum_kernel,
        grid=(bsz, s // tc),
        in_specs=[pl.BlockSpec((1, tc, w), lambda b, i: (b, i, 0)),
                  pl.BlockSpec((1, w), lambda b, i: (0, 0))],
        out_specs=pl.BlockSpec((1, tc, w), lambda b, i: (b, i, 0)),
        out_shape=jax.ShapeDtypeStruct((bsz, s, w), F32),
        scratch_shapes=[pltpu.VMEM((1, w), F32)],
        compiler_params=_params(("parallel", "arbitrary")),
        name="forget_cumsum",
    )(gate_logits, b_f_row)


LOG2E = math.log2(math.e)
FOX_HEADS_PER_STEP = 8
FOX_BIAS_TERMS = 3
FOX_BLOCK = 512


def _fox_kernel(qi_ref, ki_ref, q_ref, qc_ref, k_ref, kc_ref, rq_ref, rk_ref, vt_ref, o_ref,
                m_ref, l_ref, acc_ref, *, tb):
    step_id = pl.program_id(2)
    qi = qi_ref[step_id]
    ki = ki_ref[step_id]
    hd = FOX_HEAD_DIM

    @pl.when(ki == 0)
    def _():
        m_ref[...] = jnp.full_like(m_ref, NEG_INF)
        l_ref[...] = jnp.zeros_like(l_ref)
        acc_ref[...] = jnp.zeros_like(acc_ref)

    def step(masked, last):
        for g in range(FOX_HEADS_PER_STEP):
            rows = slice(g * hd, (g + 1) * hd)
            k_aug = jnp.concatenate([k_ref[0, :, rows], kc_ref[0, :, rows]], axis=1)
            q_aug = jnp.concatenate([q_ref[0, :, rows], qc_ref[0, :, rows]], axis=1)
            st = lax.dot_general(k_aug, q_aug, (((1,), (1,)), ((), ())),
                                 preferred_element_type=F32)
            if masked:
                key = lax.broadcasted_iota(jnp.int32, (tb, tb), 0)
                qry = lax.broadcasted_iota(jnp.int32, (tb, tb), 1)
                st = jnp.where(key <= qry, st, NEG_INF)
            delta = rq_ref[0, 0, 0, g:g + 1, :] - rk_ref[0, 0, 0, g:g + 1, :]
            m_prev = m_ref[g]
            m_new = jnp.maximum(m_prev, jnp.max(st, axis=0, keepdims=True) + delta)
            alpha = jnp.exp2(m_prev - m_new)
            p = jnp.exp2(st + (delta - m_new))
            l_new = alpha * l_ref[g] + jnp.sum(p, axis=0, keepdims=True)
            acc_new = alpha * acc_ref[g] + jnp.dot(vt_ref[rows, :], p.astype(BF16),
                                                   preferred_element_type=F32)
            if last:
                o_ref[rows, :] = (acc_new / l_new).astype(o_ref.dtype)
            else:
                m_ref[g] = m_new
                l_ref[g] = l_new
                acc_ref[g] = acc_new

    @pl.when(ki < qi)
    def _():
        step(False, False)

    @pl.when(ki == qi)
    def _():
        step(True, True)


def fox_attention(q, qc, k, kc, ref, vt):
    bsz, s, d = q.shape
    nh = d // FOX_HEAD_DIM
    hps = FOX_HEADS_PER_STEP
    assert nh % hps == 0
    tb = min(FOX_BLOCK, s)
    nb = s // tb
    pairs = [(qi, ki) for qi in range(nb) for ki in range(qi + 1)]
    qi_arr = jnp.asarray([p[0] for p in pairs], jnp.int32)
    ki_arr = jnp.asarray([p[1] for p in pairs], jnp.int32)
    wide = hps * FOX_HEAD_DIM
    grid_spec = pltpu.PrefetchScalarGridSpec(
        num_scalar_prefetch=2,
        grid=(bsz, nh // hps, len(pairs)),
        in_specs=[
            pl.BlockSpec((1, tb, wide), lambda b, h, i, qa, ka: (b, qa[i], h)),
            pl.BlockSpec((1, tb, wide), lambda b, h, i, qa, ka: (b, qa[i], h)),
            pl.BlockSpec((1, tb, wide), lambda b, h, i, qa, ka: (b, ka[i], h)),
            pl.BlockSpec((1, tb, wide), lambda b, h, i, qa, ka: (b, ka[i], h)),
            pl.BlockSpec((1, 1, 1, hps, tb), lambda b, h, i, qa, ka: (b, h, qa[i], 0, 0)),
            pl.BlockSpec((1, 1, 1, hps, tb), lambda b, h, i, qa, ka: (b, h, ka[i], 0, 0)),
            pl.BlockSpec((hps * FOX_HEAD_DIM, tb), lambda b, h, i, qa, ka: (h, b * nb + ka[i])),
        ],
        out_specs=pl.BlockSpec((hps * FOX_HEAD_DIM, tb), lambda b, h, i, qa, ka: (h, b * nb + qa[i])),
        scratch_shapes=[pltpu.VMEM((hps, 1, tb), F32), pltpu.VMEM((hps, 1, tb), F32),
                        pltpu.VMEM((hps, FOX_HEAD_DIM, tb), F32)],
    )
    return pl.pallas_call(
        functools.partial(_fox_kernel, tb=tb),
        grid_spec=grid_spec,
        out_shape=jax.ShapeDtypeStruct(vt.shape, BF16),
        compiler_params=_params(("parallel", "parallel", "arbitrary")),
        name="fox_attention",
    )(qi_arr, ki_arr, q, qc, k, kc, ref, ref, vt)


def _rope_rows(a, cos, sin):
    half = a.shape[-1] // 2
    a1, a2 = a[:, :half], a[:, half:]
    return jnp.concatenate([a1 * cos - a2 * sin, a1 * sin + a2 * cos], axis=-1)


def _retention_kernel(q_ref, k_ref, v_ref, gate_ref, cos_ref, sin_ref, dmat_ref, qd_ref, kd_ref,
                      bd_ref, gn_ref, o_ref, state_ref):
    @pl.when(pl.program_id(2) == 0)
    def _():
        state_ref[...] = jnp.zeros_like(state_ref)

    cos = cos_ref[...]
    sin = sin_ref[...]
    q = _rope_rows(q_ref[0], cos, sin)
    k = _rope_rows(k_ref[0], cos, sin) * (RET_QK_DIM ** -0.5)
    v = v_ref[0]
    scores = lax.dot_general(q.astype(BF16), k.astype(BF16), (((1,), (1,)), ((), ())),
                             preferred_element_type=F32) * dmat_ref[0]
    out = jnp.dot(scores.astype(BF16), v, preferred_element_type=F32)
    state = state_ref[...]
    out = out + jnp.dot((q * qd_ref[0]).astype(BF16), state.astype(BF16),
                        preferred_element_type=F32)
    kdt = jnp.transpose(k * kd_ref[0]).astype(BF16)
    state_ref[...] = state * bd_ref[0] + jnp.dot(kdt, v, preferred_element_type=F32)
    mu = jnp.mean(out, axis=-1, keepdims=True)
    oc = out - mu
    var = jnp.mean(oc * oc, axis=-1, keepdims=True)
    o = oc * lax.rsqrt(var + LN_EPS) * gn_ref[...]
    g = gate_ref[0].astype(F32)
    o_ref[0] = (g * jax.nn.sigmoid(g) * o).astype(o_ref.dtype)


def _retention_tables(nh, blk):
    log_gamma = jnp.log(1.0 - jnp.exp2(-5.0 - jnp.arange(nh, dtype=F32)))
    t = jnp.arange(blk, dtype=F32)
    chunk_id = jnp.arange(blk) // CHUNK
    dist = jnp.abs(t[:, None] - t[None, :])
    allowed = chunk_id[None, :] <= chunk_id[:, None]
    dmat = jnp.where(allowed[None], jnp.exp(log_gamma[:, None, None] * dist[None]), 0.0)
    qd = jnp.exp(log_gamma[:, None] * (t[None, :] + 1.0))[..., None]
    kd = jnp.exp(log_gamma[:, None] * (blk - 1.0 - t[None, :]))[..., None]
    bd = jnp.exp(log_gamma * blk).reshape(nh, 1, 1)
    return dmat.astype(F32), qd, kd, bd


def retention(qk, vg, cos, sin, gn_g, *, blk=256):
    bsz, s, w = qk.shape
    nh = w // (2 * RET_QK_DIM)
    blk = min(blk, s)
    dmat, qd, kd, bd = _retention_tables(nh, blk)
    half = RET_QK_DIM // 2
    return pl.pallas_call(
        _retention_kernel,
        grid=(bsz, nh, s // blk),
        in_specs=[
            pl.BlockSpec((1, blk, RET_QK_DIM), lambda b, h, i: (b, i, h)),
            pl.BlockSpec((1, blk, RET_QK_DIM), lambda b, h, i: (b, i, nh + h)),
            pl.BlockSpec((1, blk, RET_V_DIM), lambda b, h, i: (b, i, h)),
            pl.BlockSpec((1, blk, RET_V_DIM), lambda b, h, i: (b, i, nh + h)),
            pl.BlockSpec((blk, half), lambda b, h, i: (i, 0)),
            pl.BlockSpec((blk, half), lambda b, h, i: (i, 0)),
            pl.BlockSpec((1, blk, blk), lambda b, h, i: (h, 0, 0)),
            pl.BlockSpec((1, blk, 1), lambda b, h, i: (h, 0, 0)),
            pl.BlockSpec((1, blk, 1), lambda b, h, i: (h, 0, 0)),
            pl.BlockSpec((1, 1, 1), lambda b, h, i: (h, 0, 0)),
            pl.BlockSpec((1, RET_V_DIM), lambda b, h, i: (0, h)),
        ],
        out_specs=pl.BlockSpec((1, blk, RET_V_DIM), lambda b, h, i: (b, i, h)),
        out_shape=jax.ShapeDtypeStruct((bsz, s, nh * RET_V_DIM), BF16),
        scratch_shapes=[pltpu.VMEM((RET_QK_DIM, RET_V_DIM), F32)],
        compiler_params=_params(("parallel", "parallel", "arbitrary")),
        name="retention",
    )(qk, qk, vg, vg, cos, sin, dmat, qd, kd, bd, gn_g.reshape(1, -1))


def _peer_scores_kernel(x_ref, wq_ref, k1_ref, k2_ref, s1_ref, s2_ref):
    q = jnp.dot(x_ref[...], wq_ref[...], preferred_element_type=F32).astype(BF16)
    nh = s1_ref.shape[0]
    nt = (((1,), (1,)), ((), ()))
    for h in range(nh):
        lo = h * PEER_QUERY_DIM
        s1_ref[h] = lax.dot_general(k1_ref[...], q[:, lo:lo + PEER_HALF], nt,
                                    preferred_element_type=F32)
        s2_ref[h] = lax.dot_general(k2_ref[...], q[:, lo + PEER_HALF:lo + PEER_QUERY_DIM], nt,
                                    preferred_element_type=F32)


def peer_scores(xb, wq, k1, k2, *, tm=256):
    t, d = xb.shape
    nh = wq.shape[1] // PEER_QUERY_DIM
    tm = min(tm, t)
    shp = jax.ShapeDtypeStruct((nh, PEER_N_KEYS, t), F32)
    return pl.pallas_call(
        _peer_scores_kernel,
        grid=(t // tm,),
        in_specs=[pl.BlockSpec((tm, d), lambda i: (i, 0)),
                  pl.BlockSpec(wq.shape, lambda i: (0, 0)),
                  pl.BlockSpec(k1.shape, lambda i: (0, 0)),
                  pl.BlockSpec(k2.shape, lambda i: (0, 0))],
        out_specs=[pl.BlockSpec((nh, PEER_N_KEYS, tm), lambda i: (0, 0, i)),
                   pl.BlockSpec((nh, PEER_N_KEYS, tm), lambda i: (0, 0, i))],
        out_shape=[shp, shp],
        compiler_params=_params(("parallel",)),
        name="peer_scores",
    )(xb, wq, k1, k2)


_PEER_CANDS = tuple((i, j) for i in range(PEER_TOPK) for j in range(PEER_TOPK)
                    if (i + 1) * (j + 1) <= PEER_TOPK)


def _top16_keys(s):
    nh, _, tt = s.shape
    key = lax.broadcasted_iota(jnp.int32, s.shape, 1)
    sub = lax.broadcasted_iota(jnp.int32, (nh, tt), 0)

    def pack_heads(a):
        out = jnp.broadcast_to(a[0], (nh, tt))
        for h in range(1, nh):
            out = jnp.where(sub == h, jnp.broadcast_to(a[h], (nh, tt)), out)
        return out

    vals, idxs = [], []
    for _ in range(PEER_TOPK):
        m = jnp.max(s, axis=1, keepdims=True)
        idx = jnp.min(jnp.where(s == m, key, PEER_N_KEYS), axis=1, keepdims=True)
        s = jnp.where(key == idx, NEG_INF, s)
        vals.append(pack_heads(m))
        idxs.append(pack_heads(idx))
    return vals, idxs


def _peer_topk_kernel(s1_ref, s2_ref, eid_ref, g_ref):
    v1, i1 = _top16_keys(s1_ref[...])
    v2, i2 = _top16_keys(s2_ref[...])
    cand = [v1[i] + v2[j] for (i, j) in _PEER_CANDS]
    ceid = [i1[i] * PEER_N_KEYS + i2[j] for (i, j) in _PEER_CANDS]
    ncand = len(_PEER_CANDS)
    scores, eids = [], []
    for _ in range(PEER_TOPK):
        m = functools.reduce(jnp.maximum, cand)
        pos = functools.reduce(jnp.minimum,
                               [jnp.where(cand[c] == m, c, ncand) for c in range(ncand)])
        e = jnp.zeros_like(ceid[0])
        for c in range(ncand):
            hit = pos == c
            e = jnp.where(hit, ceid[c], e)
            cand[c] = jnp.where(hit, NEG_INF, cand[c])
        scores.append(m)
        eids.append(e)
    ex = [jnp.exp(sc - scores[0]) for sc in scores]
    denom = functools.reduce(jnp.add, ex)
    eid_ref[...] = jnp.transpose(jnp.concatenate(eids, axis=0))
    g_ref[...] = jnp.transpose(jnp.concatenate([e / denom for e in ex], axis=0))


def peer_topk(s1, s2, *, tt=128):
    nh, nk, t = s1.shape
    tt = min(tt, t)
    return pl.pallas_call(
        _peer_topk_kernel,
        grid=(t // tt,),
        in_specs=[pl.BlockSpec((nh, nk, tt), lambda i: (0, 0, i)),
                  pl.BlockSpec((nh, nk, tt), lambda i: (0, 0, i))],
        out_specs=[pl.BlockSpec((tt, PEER_TOPK * nh), lambda i: (i, 0)),
                   pl.BlockSpec((tt, PEER_TOPK * nh), lambda i: (i, 0))],
        out_shape=[jax.ShapeDtypeStruct((t, PEER_TOPK * nh), jnp.int32),
                   jax.ShapeDtypeStruct((t, PEER_TOPK * nh), F32)],
        compiler_params=_params(("parallel",)),
        name="peer_topk",
    )(s1, s2)


PEER_GROUP = 16
PEER_TOKENS_PER_STEP = 32


def _split_bf16(a):
    hi = a.astype(BF16)
    lo = (a - hi.astype(F32)).astype(BF16)
    return jnp.concatenate([hi, lo], axis=0)


def _diag_mask():
    sub = lax.broadcasted_iota(jnp.int32, (V7X_SUBLANES, V7X_LANES), 0)
    lane = lax.broadcasted_iota(jnp.int32, (V7X_SUBLANES, V7X_LANES), 1)
    return (lane & (V7X_SUBLANES - 1)) == sub


def _gather_tiles(tab_ref, eid_ref, first):
    tiles = []
    for lo in range(0, PEER_GROUP, V7X_SUBLANES):
        part = eid_ref.at[pl.ds(first + lo, V7X_SUBLANES)]
        tiles += [tab_ref[part[j]] for j in range(V7X_SUBLANES)]
    return jnp.concatenate(tiles, axis=0)


def _for_each_token(tb, npairs, body):
    def trip(i, carry):
        for u in range(PEER_TOKENS_PER_STEP):
            t = i * PEER_TOKENS_PER_STEP + u
            body(t, t * npairs)
        return carry

    lax.fori_loop(0, tb // PEER_TOKENS_PER_STEP, trip, 0)


def _peer_up_kernel(eid_ref, x_ref, tab_ref, o_ref, zs_ref, *, npairs):
    tb = x_ref.shape[0]
    ngroups = npairs // PEER_GROUP
    diag = _diag_mask()
    nt = (((1,), (1,)), ((), ()))
    sub = lax.broadcasted_iota(jnp.int32, (V7X_SUBLANES, V7X_LANES), 0)
    octets = PEER_TOKENS_PER_STEP // V7X_SUBLANES

    def trip(i, carry):
        for o in range(octets):
            tiles = [jnp.zeros((V7X_SUBLANES, V7X_LANES), F32) for _ in range(ngroups)]
            for u in range(V7X_SUBLANES):
                t = (i * octets + o) * V7X_SUBLANES + u
                lhs = _split_bf16(x_ref[t])
                for k in range(ngroups):
                    g = _gather_tiles(tab_ref, eid_ref, t * npairs + k * PEER_GROUP)
                    z = lax.dot_general(lhs, g, nt, preferred_element_type=F32)
                    zd = jnp.where(diag, z[:V7X_SUBLANES] + z[V7X_SUBLANES:], 0.0)
                    for shift in (1, 2, 4):
                        zd = zd + pltpu.roll(zd, shift=shift, axis=0)
                    tiles[k] = jnp.where(sub == u, zd, tiles[k])
            for k in range(ngroups):
                zs_ref[k, i * octets + o] = tiles[k]
        return carry

    lax.fori_loop(0, tb // PEER_TOKENS_PER_STEP, trip, 0)

    row = lax.broadcasted_iota(jnp.int32, (V7X_LANES, npairs), 0)
    col = lax.broadcasted_iota(jnp.int32, (V7X_LANES, npairs), 1)
    acc = jnp.zeros((tb, npairs), F32)
    for k in range(ngroups):
        zr = zs_ref[k].reshape(tb, V7X_LANES)
        sel = jnp.where(col == k * PEER_GROUP + row // V7X_SUBLANES, 1.0, 0.0).astype(BF16)
        zh = zr.astype(BF16)
        zl = (zr - zh.astype(F32)).astype(BF16)
        acc = acc + jnp.dot(zh, sel, preferred_element_type=F32) + jnp.dot(zl, sel, preferred_element_type=F32)
    o_ref[...] = acc


def peer_up(eid_tok, x4, utab, *, tb=128):
    t, npairs = eid_tok.shape
    nexp, nhalf, r, lanes = utab.shape
    tb = min(tb, t)
    assert r == V7X_SUBLANES and lanes == V7X_LANES and npairs % PEER_GROUP == 0
    return pl.pallas_call(
        functools.partial(_peer_up_kernel, npairs=npairs),
        grid=(nhalf, t // tb),
        in_specs=[pl.BlockSpec((tb * npairs,), lambda c, i: (i,), memory_space=pltpu.SMEM),
                  pl.BlockSpec((tb, None, r, lanes), lambda c, i: (i, c, 0, 0)),
                  pl.BlockSpec((nexp, None, r, lanes), lambda c, i: (0, c, 0, 0),
                               pipeline_mode=pl.Buffered(1))],
        out_specs=pl.BlockSpec((None, tb, npairs), lambda c, i: (c, i, 0)),
        out_shape=jax.ShapeDtypeStruct((nhalf, t, npairs), F32),
        scratch_shapes=[pltpu.VMEM((npairs // PEER_GROUP, tb // V7X_SUBLANES, r, lanes), F32)],
        compiler_params=_params(("arbitrary", "arbitrary")),
        name="peer_up",
    )(eid_tok.reshape(-1), x4, utab)


def _peer_act_kernel(h_ref, g_ref, w_ref):
    h = h_ref[0] + h_ref[1]
    gelu = 0.5 * h * (1.0 + lax.erf(h * (2.0 ** -0.5)))
    w_ref[...] = g_ref[...] * gelu


def peer_act(h2, g, *, tt=512):
    _, t, npairs = h2.shape
    tt = min(tt, t)
    return pl.pallas_call(
        _peer_act_kernel,
        grid=(t // tt,),
        in_specs=[pl.BlockSpec((2, tt, npairs), lambda i: (0, i, 0)),
                  pl.BlockSpec((tt, npairs), lambda i: (i, 0))],
        out_specs=pl.BlockSpec((tt, npairs), lambda i: (i, 0)),
        out_shape=jax.ShapeDtypeStruct((t, npairs), F32),
        compiler_params=_params(("parallel",)),
        name="peer_act",
    )(h2, g)


def _peer_down_kernel(eid_ref, w_ref, tab_ref, o_ref, *, npairs):
    tb = o_ref.shape[0]
    ngroups = npairs // PEER_GROUP
    diag = _diag_mask()

    def token(t, first):
        wall = w_ref[t]
        acc = jnp.zeros((2 * V7X_SUBLANES, V7X_LANES), F32)
        for k in range(ngroups):
            g = _gather_tiles(tab_ref, eid_ref, first + k * PEER_GROUP)
            wk = jnp.broadcast_to(wall[k:k + 1, :], (V7X_SUBLANES, V7X_LANES))
            lhs = _split_bf16(jnp.where(diag, wk, 0.0))
            acc = acc + jnp.dot(lhs, g, preferred_element_type=F32)
        o_ref[t] = acc[:V7X_SUBLANES] + acc[V7X_SUBLANES:]

    _for_each_token(tb, npairs, token)


def peer_down(eid_tok, w_rep, vtab, *, tb=128):
    t, npairs = eid_tok.shape
    nexp, nhalf, r, lanes = vtab.shape
    tb = min(tb, t)
    assert r == V7X_SUBLANES and lanes == V7X_LANES and npairs // PEER_GROUP == V7X_SUBLANES
    return pl.pallas_call(
        functools.partial(_peer_down_kernel, npairs=npairs),
        grid=(nhalf, t // tb),
        in_specs=[pl.BlockSpec((tb * npairs,), lambda c, i: (i,), memory_space=pltpu.SMEM),
                  pl.BlockSpec((tb, npairs // PEER_GROUP, lanes), lambda c, i: (i, 0, 0)),
                  pl.BlockSpec((nexp, None, r, lanes), lambda c, i: (0, c, 0, 0),
                               pipeline_mode=pl.Buffered(1))],
        out_specs=pl.BlockSpec((tb, None, r, lanes), lambda c, i: (i, c, 0, 0)),
        out_shape=jax.ShapeDtypeStruct((t, nhalf, r, lanes), F32),
        compiler_params=_params(("arbitrary", "arbitrary")),
        name="peer_down",
    )(eid_tok.reshape(-1), w_rep, vtab)


def _expert_table(w):
    nexp, d = w.shape
    r = d // (2 * V7X_LANES)
    return w.astype(BF16).reshape(nexp, 2, r, V7X_LANES)


def peer_ffn(x_f32, x_bf16, wq, k1, k2, u, v):
    t, d = x_f32.shape
    r = d // (2 * V7X_LANES)
    s1, s2 = peer_scores(x_bf16, wq.astype(BF16), k1.astype(BF16), k2.astype(BF16))
    eid_tok, gate = peer_topk(s1, s2)
    npairs = eid_tok.shape[1]
    h2 = peer_up(eid_tok, x_f32.reshape(t, 2, r, V7X_LANES), _expert_table(u))
    w = peer_act(h2, gate)
    w_rep = jnp.repeat(w, V7X_SUBLANES, axis=1).reshape(t, npairs // PEER_GROUP, V7X_LANES)
    y = peer_down(eid_tok, w_rep, _expert_table(v))
    return y.reshape(t, d)


def fox_layer(x_f32, x_bf16, bsz, s, w_in, b_f, w_o, ln_g, ln_b):
    t, d = x_f32.shape
    nh = d // FOX_HEAD_DIM
    wb = w_in.astype(BF16)
    q2 = matmul(x_bf16, wb[:, :d], BF16, scale=LOG2E * FOX_HEAD_DIM ** -0.5, name="fox_q")
    k = matmul(x_bf16, wb[:, d:2 * d], BF16, name="fox_k")
    vt = matmul(wb[:, 2 * d:3 * d].T, x_bf16.T, BF16, name="fox_vt")
    wg = jnp.pad(wb[:, 3 * d:], ((0, 0), (0, V7X_LANES - nh)))
    gl = matmul(x_bf16, wg, F32, tn=V7X_LANES, name="fox_gate")
    bf_row = jnp.pad(b_f.astype(F32), (0, V7X_LANES - nh)).reshape(1, V7X_LANES)
    c = forget_cumsum(gl.reshape(bsz, s, V7X_LANES), bf_row)[:, :, :nh]
    tb = min(FOX_BLOCK, s)
    nb = s // tb
    hps = FOX_HEADS_PER_STEP
    c2 = c * LOG2E
    ref = c2[:, ::tb, :]
    rel = (c2.reshape(bsz, nb, tb, nh) - ref[:, :, None, :]).reshape(t, nh)
    terms, rest = [], rel
    for _ in range(FOX_BIAS_TERMS):
        part = rest.astype(BF16)
        terms.append(part)
        rest = rest - part.astype(F32)
    cterms = jnp.stack(terms, axis=-1)
    ones = jnp.ones_like(cterms)
    pad = jnp.zeros((t, nh, FOX_HEAD_DIM - 2 * FOX_BIAS_TERMS), BF16)
    qc = jnp.concatenate([cterms, ones, pad], axis=-1).reshape(bsz, s, d)
    kc = jnp.concatenate([ones, -cterms, pad], axis=-1).reshape(bsz, s, d)
    ref_rows = jnp.broadcast_to(
        jnp.transpose(ref, (0, 2, 1)).reshape(bsz, nh // hps, hps, nb).transpose(0, 1, 3, 2)[..., None],
        (bsz, nh // hps, nb, hps, tb))
    ot = fox_attention(q2.reshape(bsz, s, d), qc, k.reshape(bsz, s, d), kc, ref_rows, vt)
    return proj_residual_ln(ot.T, w_o.astype(BF16), x_f32, ln_g, ln_b, name="fox_out_ln")


def retention_layer(x_f32, x_bf16, bsz, s, w_in, gn_g, w_o, ln_g, ln_b):
    t, d = x_f32.shape
    nh = d // RET_QK_DIM
    qkw = 2 * nh * RET_QK_DIM
    wb = w_in.astype(BF16)
    qk = matmul(x_bf16, wb[:, :qkw], F32, name="ret_qk")
    vg = matmul(x_bf16, wb[:, qkw:], BF16, name="ret_vg")
    half = RET_QK_DIM // 2
    inv_freq = ROPE_BASE ** (-jnp.arange(half, dtype=F32) / half)
    ang = jnp.arange(s, dtype=F32)[:, None] * inv_freq[None, :]
    o = retention(qk.reshape(bsz, s, -1), vg.reshape(bsz, s, -1), jnp.cos(ang), jnp.sin(ang), gn_g)
    return proj_residual_ln(o.reshape(t, -1), w_o.astype(BF16), x_f32, ln_g, ln_b, name="ret_out_ln")


def kernel(x, l0_fox_w_in, l0_fox_b_f, l0_fox_w_o, l0_ln1_g, l0_ln1_b, l0_peer_wq, l0_peer_k1, l0_peer_k2, l0_peer_u, l0_peer_v, l0_ln2_g, l0_ln2_b, l1_ret_w_in, l1_ret_gn_g, l1_ret_w_o, l1_ln1_g, l1_ln1_b, l1_peer_wq, l1_peer_k1, l1_peer_k2, l1_peer_u, l1_peer_v, l1_ln2_g, l1_ln2_b):
    bsz, s, d = x.shape
    t = bsz * s
    xf = x.reshape(t, d).astype(F32)
    xb = xf.astype(BF16)

    xf, xb = fox_layer(xf, xb, bsz, s, l0_fox_w_in, l0_fox_b_f, l0_fox_w_o, l0_ln1_g, l0_ln1_b)
    y = peer_ffn(xf, xb, l0_peer_wq, l0_peer_k1, l0_peer_k2, l0_peer_u, l0_peer_v)
    xf, xb = residual_ln(y, xf, l0_ln2_g, l0_ln2_b, name="peer0_ln")

    xf, xb = retention_layer(xf, xb, bsz, s, l1_ret_w_in, l1_ret_gn_g, l1_ret_w_o, l1_ln1_g, l1_ln1_b)
    y = peer_ffn(xf, xb, l1_peer_wq, l1_peer_k1, l1_peer_k2, l1_peer_u, l1_peer_v)
    xf, _ = residual_ln(y, xf, l1_ln2_g, l1_ln2_b, name="peer1_ln")
    return xf.reshape(bsz, s, d).astype(x.dtype)
```

```python
import functools
import math

import jax
import jax.numpy as jnp
from jax import lax
from jax.experimental import pallas as pl
from jax.experimental.pallas import tpu as pltpu

DEPTH = 2
CHUNK = 64
FOX_HEAD_DIM = 128
RET_QK_DIM = 256
RET_V_DIM = 512
ROPE_BASE = 10000.0
PEER_N_KEYS = 128
PEER_QUERY_DIM = 256
PEER_HALF = PEER_QUERY_DIM // 2
PEER_TOPK = 16
DN_ALPHA = (2 * DEPTH) ** 0.25
LN_EPS = 1e-5

V7X_LANES = 128
V7X_SUBLANES = 8
V7X_VMEM_BYTES = 64 * 1024 * 1024
VMEM_LIMIT = V7X_VMEM_BYTES - 6 * 1024 * 1024

F32 = jnp.float32
BF16 = jnp.bfloat16
NEG_INF = float("-inf")


def _params(sem, vmem=None):
    return pltpu.CompilerParams(dimension_semantics=sem, vmem_limit_bytes=vmem or VMEM_LIMIT)


def _mm_kernel(a_ref, b_ref, o_ref, *, scale, b_transposed):
    contract_b = 1 if b_transposed else 0
    acc = lax.dot_general(a_ref[...], b_ref[...], (((1,), (contract_b,)), ((), ())),
                          preferred_element_type=F32)
    if scale is not None:
        acc = acc * scale
    o_ref[...] = acc.astype(o_ref.dtype)


def matmul(a, b, out_dtype, *, scale=None, tm=1024, tn=1024, b_transposed=False, name="mm"):
    m, k = a.shape
    n = b.shape[0] if b_transposed else b.shape[1]
    tm = min(tm, m)
    tn = min(tn, n)
    b_spec = pl.BlockSpec((tn, k), lambda j, i: (j, 0)) if b_transposed else pl.BlockSpec((k, tn), lambda j, i: (0, j))
    return pl.pallas_call(
        functools.partial(_mm_kernel, scale=scale, b_transposed=b_transposed),
        grid=(n // tn, m // tm),
        in_specs=[pl.BlockSpec((tm, k), lambda j, i: (i, 0)),
                  b_spec],
        out_specs=pl.BlockSpec((tm, tn), lambda j, i: (i, j)),
        out_shape=jax.ShapeDtypeStruct((m, n), out_dtype),
        compiler_params=_params(("parallel", "parallel")),
        name=name,
    )(a, b)


def _layer_norm_rows(y, g, b):
    mu = jnp.mean(y, axis=-1, keepdims=True)
    yc = y - mu
    var = jnp.mean(yc * yc, axis=-1, keepdims=True)
    return yc * lax.rsqrt(var + LN_EPS) * g + b


def _proj_ln_kernel(a_ref, w_ref, x_ref, g_ref, b_ref, of_ref, ob_ref, *, a_transposed):
    contract_a = 0 if a_transposed else 1
    y = lax.dot_general(a_ref[...], w_ref[...], (((contract_a,), (0,)), ((), ())),
                        preferred_element_type=F32) + DN_ALPHA * x_ref[...]
    yn = _layer_norm_rows(y, g_ref[...], b_ref[...])
    of_ref[...] = yn
    ob_ref[...] = yn.astype(BF16)


def proj_residual_ln(a, w, x, g, b, *, tm=256, a_transposed=False, name="proj_ln"):
    k, d = w.shape
    m = x.shape[0]
    tm = min(tm, m)
    a_spec = pl.BlockSpec((k, tm), lambda i: (0, i)) if a_transposed else pl.BlockSpec((tm, k), lambda i: (i, 0))
    return pl.pallas_call(
        functools.partial(_proj_ln_kernel, a_transposed=a_transposed),
        grid=(m // tm,),
        in_specs=[a_spec,
                  pl.BlockSpec((k, d), lambda i: (0, 0)),
                  pl.BlockSpec((tm, d), lambda i: (i, 0)),
                  pl.BlockSpec((1, d), lambda i: (0, 0)),
                  pl.BlockSpec((1, d), lambda i: (0, 0))],
        out_specs=[pl.BlockSpec((tm, d), lambda i: (i, 0)),
                   pl.BlockSpec((tm, d), lambda i: (i, 0))],
        out_shape=[jax.ShapeDtypeStruct((m, d), F32), jax.ShapeDtypeStruct((m, d), BF16)],
        compiler_params=_params(("parallel",)),
        name=name,
    )(a, w, x, g.reshape(1, d), b.reshape(1, d))


def _residual_ln_kernel(y_ref, x_ref, g_ref, b_ref, of_ref, ob_ref):
    y = y_ref[...] + DN_ALPHA * x_ref[...]
    yn = _layer_norm_rows(y, g_ref[...], b_ref[...])
    of_ref[...] = yn
    ob_ref[...] = yn.astype(BF16)


def residual_ln(y, x, g, b, *, tm=512, name="res_ln"):
    m, d = x.shape
    tm = min(tm, m)
    return pl.pallas_call(
        _residual_ln_kernel,
        grid=(m // tm,),
        in_specs=[pl.BlockSpec((tm, d), lambda i: (i, 0)),
                  pl.BlockSpec((tm, d), lambda i: (i, 0)),
                  pl.BlockSpec((1, d), lambda i: (0, 0)),
                  pl.BlockSpec((1, d), lambda i: (0, 0))],
        out_specs=[pl.BlockSpec((tm, d), lambda i: (i, 0)),
                   pl.BlockSpec((tm, d), lambda i: (i, 0))],
        out_shape=[jax.ShapeDtypeStruct((m, d), F32), jax.ShapeDtypeStruct((m, d), BF16)],
        compiler_params=_params(("parallel",)),
        name=name,
    )(y, x, g.reshape(1, d), b.reshape(1, d))


def _forget_cumsum_kernel(g_ref, bf_ref, c_ref, carry_ref):
    @pl.when(pl.program_id(1) == 0)
    def _():
        carry_ref[...] = jnp.zeros_like(carry_ref)

    z = g_ref[0] + bf_ref[...]
    lf = jnp.minimum(z, 0.0) - jnp.log1p(jnp.exp(-jnp.abs(z)))
    tc = lf.shape[0]
    row = lax.broadcasted_iota(jnp.int32, (tc, tc), 0)
    col = lax.broadcasted_iota(jnp.int32, (tc, tc), 1)
    tri = jnp.where(row >= col, 1.0, 0.0).astype(BF16)
    hi = lf.astype(BF16)
    lo = (lf - hi.astype(F32)).astype(BF16)
    cs = (jnp.dot(tri, hi, preferred_element_type=F32)
          + jnp.dot(tri, lo, preferred_element_type=F32))
    c = cs + carry_ref[...]
    c_ref[0] = c
    carry_ref[...] = c[tc - 1:tc, :]


def forget_cumsum(gate_logits, b_f_row, *, tc=512):
    bsz, s, w = gate_logits.shape
    tc = min(tc, s)
    return pl.pallas_call(
        _forget_cumsum_kernel,
        grid=(bsz, s // tc),
        in_specs=[pl.BlockSpec((1, tc, w), lambda b, i: (b, i, 0)),
                  pl.BlockSpec((1, w), lambda b, i: (0, 0))],
        out_specs=pl.BlockSpec((1, tc, w), lambda b, i: (b, i, 0)),
        out_shape=jax.ShapeDtypeStruct((bsz, s, w), F32),
        scratch_shapes=[pltpu.VMEM((1, w), F32)],
        compiler_params=_params(("parallel", "arbitrary")),
        name="forget_cumsum",
    )(gate_logits, b_f_row)


LOG2E = math.log2(math.e)
FOX_HEADS_PER_STEP = 8
FOX_BIAS_TERMS = 3
FOX_BLOCK = 512


def _fox_kernel(qi_ref, ki_ref, q_ref, qc_ref, k_ref, kc_ref, rq_ref, rk_ref, vt_ref, o_ref,
                m_ref, l_ref, acc_ref, *, tb):
    step_id = pl.program_id(2)
    qi = qi_ref[step_id]
    ki = ki_ref[step_id]
    hd = FOX_HEAD_DIM

    @pl.when(ki == 0)
    def _():
        m_ref[...] = jnp.full_like(m_ref, NEG_INF)
        l_ref[...] = jnp.zeros_like(l_ref)
        acc_ref[...] = jnp.zeros_like(acc_ref)

    def step(masked, last):
        for g in range(FOX_HEADS_PER_STEP):
            rows = slice(g * hd, (g + 1) * hd)
            k_aug = jnp.concatenate([k_ref[0, :, rows], kc_ref[0, :, rows]], axis=1)
            q_aug = jnp.concatenate([q_ref[0, :, rows], qc_ref[0, :, rows]], axis=1)
            st = lax.dot_general(k_aug, q_aug, (((1,), (1,)), ((), ())),
                                 preferred_element_type=F32)
            if masked:
                key = lax.broadcasted_iota(jnp.int32, (tb, tb), 0)
                qry = lax.broadcasted_iota(jnp.int32, (tb, tb), 1)
                st = jnp.where(key <= qry, st, NEG_INF)
            delta = rq_ref[0, 0, 0, g:g + 1, :] - rk_ref[0, 0, 0, g:g + 1, :]
            m_prev = m_ref[g]
            m_new = jnp.maximum(m_prev, jnp.max(st, axis=0, keepdims=True) + delta)
            alpha = jnp.exp2(m_prev - m_new)
            p = jnp.exp2(st + (delta - m_new))
            l_new = alpha * l_ref[g] + jnp.sum(p, axis=0, keepdims=True)
            acc_new = alpha * acc_ref[g] + jnp.dot(vt_ref[rows, :], p.astype(BF16),
                                                   preferred_element_type=F32)
            if last:
                o_ref[rows, :] = (acc_new / l_new).astype(o_ref.dtype)
            else:
                m_ref[g] = m_new
                l_ref[g] = l_new
                acc_ref[g] = acc_new

    @pl.when(ki < qi)
    def _():
        step(False, False)

    @pl.when(ki == qi)
    def _():
        step(True, True)


def fox_attention(q, qc, k, kc, ref, vt):
    bsz, s, d = q.shape
    nh = d // FOX_HEAD_DIM
    hps = FOX_HEADS_PER_STEP
    assert nh % hps == 0
    tb = min(FOX_BLOCK, s)
    nb = s // tb
    pairs = [(qi, ki) for qi in range(nb) for ki in range(qi + 1)]
    qi_arr = jnp.asarray([p[0] for p in pairs], jnp.int32)
    ki_arr = jnp.asarray([p[1] for p in pairs], jnp.int32)
    wide = hps * FOX_HEAD_DIM
    grid_spec = pltpu.PrefetchScalarGridSpec(
        num_scalar_prefetch=2,
        grid=(bsz, nh // hps, len(pairs)),
        in_specs=[
            pl.BlockSpec((1, tb, wide), lambda b, h, i, qa, ka: (b, qa[i], h)),
            pl.BlockSpec((1, tb, wide), lambda b, h, i, qa, ka: (b, qa[i], h)),
            pl.BlockSpec((1, tb, wide), lambda b, h, i, qa, ka: (b, ka[i], h)),
            pl.BlockSpec((1, tb, wide), lambda b, h, i, qa, ka: (b, ka[i], h)),
            pl.BlockSpec((1, 1, 1, hps, tb), lambda b, h, i, qa, ka: (b, h, qa[i], 0, 0)),
            pl.BlockSpec((1, 1, 1, hps, tb), lambda b, h, i, qa, ka: (b, h, ka[i], 0, 0)),
            pl.BlockSpec((hps * FOX_HEAD_DIM, tb), lambda b, h, i, qa, ka: (h, b * nb + ka[i])),
        ],
        out_specs=pl.BlockSpec((hps * FOX_HEAD_DIM, tb), lambda b, h, i, qa, ka: (h, b * nb + qa[i])),
        scratch_shapes=[pltpu.VMEM((hps, 1, tb), F32), pltpu.VMEM((hps, 1, tb), F32),
                        pltpu.VMEM((hps, FOX_HEAD_DIM, tb), F32)],
    )
    return pl.pallas_call(
        functools.partial(_fox_kernel, tb=tb),
        grid_spec=grid_spec,
        out_shape=jax.ShapeDtypeStruct(vt.shape, BF16),
        compiler_params=_params(("parallel", "parallel", "arbitrary")),
        name="fox_attention",
    )(qi_arr, ki_arr, q, qc, k, kc, ref, ref, vt)


def _rope_rows(a, cos, sin):
    half = a.shape[-1] // 2
    a1, a2 = a[:, :half], a[:, half:]
    return jnp.concatenate([a1 * cos - a2 * sin, a1 * sin + a2 * cos], axis=-1)


def _retention_kernel(q_ref, k_ref, v_ref, gate_ref, cos_ref, sin_ref, dmat_ref, qd_ref, kd_ref,
                      bd_ref, gn_ref, o_ref, state_ref):
    @pl.when(pl.program_id(2) == 0)
    def _():
        state_ref[...] = jnp.zeros_like(state_ref)

    cos = cos_ref[...]
    sin = sin_ref[...]
    q = _rope_rows(q_ref[0], cos, sin)
    k = _rope_rows(k_ref[0], cos, sin) * (RET_QK_DIM ** -0.5)
    v = v_ref[0]
    scores = lax.dot_general(q.astype(BF16), k.astype(BF16), (((1,), (1,)), ((), ())),
                             preferred_element_type=F32) * dmat_ref[0]
    out = jnp.dot(scores.astype(BF16), v, preferred_element_type=F32)
    state = state_ref[...]
    out = out + jnp.dot((q * qd_ref[0]).astype(BF16), state.astype(BF16),
                        preferred_element_type=F32)
    kdt = jnp.transpose(k * kd_ref[0]).astype(BF16)
    state_ref[...] = state * bd_ref[0] + jnp.dot(kdt, v, preferred_element_type=F32)
    mu = jnp.mean(out, axis=-1, keepdims=True)
    oc = out - mu
    var = jnp.mean(oc * oc, axis=-1, keepdims=True)
    o = oc * lax.rsqrt(var + LN_EPS) * gn_ref[...]
    g = gate_ref[0].astype(F32)
    o_ref[0] = (g * jax.nn.sigmoid(g) * o).astype(o_ref.dtype)


def _retention_tables(nh, blk):
    log_gamma = jnp.log(1.0 - jnp.exp2(-5.0 - jnp.arange(nh, dtype=F32)))
    t = jnp.arange(blk, dtype=F32)
    chunk_id = jnp.arange(blk) // CHUNK
    dist = jnp.abs(t[:, None] - t[None, :])
    allowed = chunk_id[None, :] <= chunk_id[:, None]
    dmat = jnp.where(allowed[None], jnp.exp(log_gamma[:, None, None] * dist[None]), 0.0)
    qd = jnp.exp(log_gamma[:, None] * (t[None, :] + 1.0))[..., None]
    kd = jnp.exp(log_gamma[:, None] * (blk - 1.0 - t[None, :]))[..., None]
    bd = jnp.exp(log_gamma * blk).reshape(nh, 1, 1)
    return dmat.astype(F32), qd, kd, bd


def retention(qk, vg, cos, sin, gn_g, *, blk=512):
    bsz, s, w = qk.shape
    nh = w // (2 * RET_QK_DIM)
    blk = min(blk, s)
    dmat, qd, kd, bd = _retention_tables(nh, blk)
    half = RET_QK_DIM // 2
    return pl.pallas_call(
        _retention_kernel,
        grid=(bsz, nh, s // blk),
        in_specs=[
            pl.BlockSpec((1, blk, RET_QK_DIM), lambda b, h, i: (b, i, h)),
            pl.BlockSpec((1, blk, RET_QK_DIM), lambda b, h, i: (b, i, nh + h)),
            pl.BlockSpec((1, blk, RET_V_DIM), lambda b, h, i: (b, i, h)),
            pl.BlockSpec((1, blk, RET_V_DIM), lambda b, h, i: (b, i, nh + h)),
            pl.BlockSpec((blk, half), lambda b, h, i: (i, 0)),
            pl.BlockSpec((blk, half), lambda b, h, i: (i, 0)),
            pl.BlockSpec((1, blk, blk), lambda b, h, i: (h, 0, 0)),
            pl.BlockSpec((1, blk, 1), lambda b, h, i: (h, 0, 0)),
            pl.BlockSpec((1, blk, 1), lambda b, h, i: (h, 0, 0)),
            pl.BlockSpec((1, 1, 1), lambda b, h, i: (h, 0, 0)),
            pl.BlockSpec((1, RET_V_DIM), lambda b, h, i: (0, h)),
        ],
        out_specs=pl.BlockSpec((1, blk, RET_V_DIM), lambda b, h, i: (b, i, h)),
        out_shape=jax.ShapeDtypeStruct((bsz, s, nh * RET_V_DIM), BF16),
        scratch_shapes=[pltpu.VMEM((RET_QK_DIM, RET_V_DIM), F32)],
        compiler_params=_params(("parallel", "parallel", "arbitrary")),
        name="retention",
    )(qk, qk, vg, vg, cos, sin, dmat, qd, kd, bd, gn_g.reshape(1, -1))


def _peer_scores_kernel(x_ref, wq_ref, k1_ref, k2_ref, s1_ref, s2_ref):
    q = jnp.dot(x_ref[...], wq_ref[...], preferred_element_type=F32).astype(BF16)
    nh = s1_ref.shape[0]
    nt = (((1,), (1,)), ((), ()))
    for h in range(nh):
        lo = h * PEER_QUERY_DIM
        s1_ref[h] = lax.dot_general(k1_ref[...], q[:, lo:lo + PEER_HALF], nt,
                                    preferred_element_type=F32)
        s2_ref[h] = lax.dot_general(k2_ref[...], q[:, lo + PEER_HALF:lo + PEER_QUERY_DIM], nt,
                                    preferred_element_type=F32)


def peer_scores(xb, wq, k1, k2, *, tm=256):
    t, d = xb.shape
    nh = wq.shape[1] // PEER_QUERY_DIM
    tm = min(tm, t)
    shp = jax.ShapeDtypeStruct((nh, PEER_N_KEYS, t), F32)
    return pl.pallas_call(
        _peer_scores_kernel,
        grid=(t // tm,),
        in_specs=[pl.BlockSpec((tm, d), lambda i: (i, 0)),
                  pl.BlockSpec(wq.shape, lambda i: (0, 0)),
                  pl.BlockSpec(k1.shape, lambda i: (0, 0)),
                  pl.BlockSpec(k2.shape, lambda i: (0, 0))],
        out_specs=[pl.BlockSpec((nh, PEER_N_KEYS, tm), lambda i: (0, 0, i)),
                   pl.BlockSpec((nh, PEER_N_KEYS, tm), lambda i: (0, 0, i))],
        out_shape=[shp, shp],
        compiler_params=_params(("parallel",)),
        name="peer_scores",
    )(xb, wq, k1, k2)


_PEER_CANDS = tuple((i, j) for i in range(PEER_TOPK) for j in range(PEER_TOPK)
                    if (i + 1) * (j + 1) <= PEER_TOPK)


def _top16_keys(s):
    nh, _, tt = s.shape
    key = lax.broadcasted_iota(jnp.int32, s.shape, 1)
    sub = lax.broadcasted_iota(jnp.int32, (nh, tt), 0)

    def pack_heads(a):
        out = jnp.broadcast_to(a[0], (nh, tt))
        for h in range(1, nh):
            out = jnp.where(sub == h, jnp.broadcast_to(a[h], (nh, tt)), out)
        return out

    vals, idxs = [], []
    for _ in range(PEER_TOPK):
        m = jnp.max(s, axis=1, keepdims=True)
        idx = jnp.min(jnp.where(s == m, key, PEER_N_KEYS), axis=1, keepdims=True)
        s = jnp.where(key == idx, NEG_INF, s)
        vals.append(pack_heads(m))
        idxs.append(pack_heads(idx))
    return vals, idxs


def _peer_topk_kernel(s1_ref, s2_ref, eid_ref, g_ref):
    v1, i1 = _top16_keys(s1_ref[...])
    v2, i2 = _top16_keys(s2_ref[...])
    cand = [v1[i] + v2[j] for (i, j) in _PEER_CANDS]
    ceid = [i1[i] * PEER_N_KEYS + i2[j] for (i, j) in _PEER_CANDS]
    ncand = len(_PEER_CANDS)
    scores, eids = [], []
    for _ in range(PEER_TOPK):
        m = functools.reduce(jnp.maximum, cand)
        pos = functools.reduce(jnp.minimum,
                               [jnp.where(cand[c] == m, c, ncand) for c in range(ncand)])
        e = jnp.zeros_like(ceid[0])
        for c in range(ncand):
            hit = pos == c
            e = jnp.where(hit, ceid[c], e)
            cand[c] = jnp.where(hit, NEG_INF, cand[c])
        scores.append(m)
        eids.append(e)
    ex = [jnp.exp(sc - scores[0]) for sc in scores]
    denom = functools.reduce(jnp.add, ex)
    eid_ref[...] = jnp.transpose(jnp.concatenate(eids, axis=0))
    g_ref[...] = jnp.transpose(jnp.concatenate([e / denom for e in ex], axis=0))


def peer_topk(s1, s2, *, tt=128):
    nh, nk, t = s1.shape
    tt = min(tt, t)
    return pl.pallas_call(
        _peer_topk_kernel,
        grid=(t // tt,),
        in_specs=[pl.BlockSpec((nh, nk, tt), lambda i: (0, 0, i)),
                  pl.BlockSpec((nh, nk, tt), lambda i: (0, 0, i))],
        out_specs=[pl.BlockSpec((tt, PEER_TOPK * nh), lambda i: (i, 0)),
                   pl.BlockSpec((tt, PEER_TOPK * nh), lambda i: (i, 0))],
        out_shape=[jax.ShapeDtypeStruct((t, PEER_TOPK * nh), jnp.int32),
                   jax.ShapeDtypeStruct((t, PEER_TOPK * nh), F32)],
        compiler_params=_params(("parallel",)),
        name="peer_topk",
    )(s1, s2)


PEER_GROUP = 16
PEER_TOKENS_PER_STEP = 32


def _split_bf16(a):
    hi = a.astype(BF16)
    lo = (a - hi.astype(F32)).astype(BF16)
    return jnp.concatenate([hi, lo], axis=0)


def _diag_mask():
    sub = lax.broadcasted_iota(jnp.int32, (V7X_SUBLANES, V7X_LANES), 0)
    lane = lax.broadcasted_iota(jnp.int32, (V7X_SUBLANES, V7X_LANES), 1)
    return (lane & (V7X_SUBLANES - 1)) == sub


def _gather_tiles(tab_ref, eid_ref, first):
    tiles = []
    for lo in range(0, PEER_GROUP, V7X_SUBLANES):
        part = eid_ref.at[pl.ds(first + lo, V7X_SUBLANES)]
        tiles += [tab_ref[part[j]] for j in range(V7X_SUBLANES)]
    return jnp.concatenate(tiles, axis=0)


def _for_each_token(tb, npairs, body):
    def trip(i, carry):
        for u in range(PEER_TOKENS_PER_STEP):
            t = i * PEER_TOKENS_PER_STEP + u
            body(t, t * npairs)
        return carry

    lax.fori_loop(0, tb // PEER_TOKENS_PER_STEP, trip, 0)


def _peer_up_kernel(eid_ref, x_ref, tab_ref, o_ref, zs_ref, *, npairs):
    tb = x_ref.shape[0]
    ngroups = npairs // PEER_GROUP
    diag = _diag_mask()
    nt = (((1,), (1,)), ((), ()))
    sub = lax.broadcasted_iota(jnp.int32, (V7X_SUBLANES, V7X_LANES), 0)
    octets = PEER_TOKENS_PER_STEP // V7X_SUBLANES

    def trip(i, carry):
        for o in range(octets):
            tiles = [jnp.zeros((V7X_SUBLANES, V7X_LANES), F32) for _ in range(ngroups)]
            for u in range(V7X_SUBLANES):
                t = (i * octets + o) * V7X_SUBLANES + u
                lhs = _split_bf16(x_ref[t])
                for k in range(ngroups):
                    g = _gather_tiles(tab_ref, eid_ref, t * npairs + k * PEER_GROUP)
                    z = lax.dot_general(lhs, g, nt, preferred_element_type=F32)
                    zd = jnp.where(diag, z[:V7X_SUBLANES] + z[V7X_SUBLANES:], 0.0)
                    for shift in (1, 2, 4):
                        zd = zd + pltpu.roll(zd, shift=shift, axis=0)
                    tiles[k] = jnp.where(sub == u, zd, tiles[k])
            for k in range(ngroups):
                zs_ref[k, i * octets + o] = tiles[k]
        return carry

    lax.fori_loop(0, tb // PEER_TOKENS_PER_STEP, trip, 0)

    row = lax.broadcasted_iota(jnp.int32, (V7X_LANES, npairs), 0)
    col = lax.broadcasted_iota(jnp.int32, (V7X_LANES, npairs), 1)
    acc = jnp.zeros((tb, npairs), F32)
    for k in range(ngroups):
        zr = zs_ref[k].reshape(tb, V7X_LANES)
        sel = jnp.where(col == k * PEER_GROUP + row // V7X_SUBLANES, 1.0, 0.0).astype(BF16)
        zh = zr.astype(BF16)
        zl = (zr - zh.astype(F32)).astype(BF16)
        acc = acc + jnp.dot(zh, sel, preferred_element_type=F32) + jnp.dot(zl, sel, preferred_element_type=F32)
    o_ref[...] = acc


def peer_up(eid_tok, x4, utab, *, tb=128):
    t, npairs = eid_tok.shape
    nexp, nhalf, r, lanes = utab.shape
    tb = min(tb, t)
    assert r == V7X_SUBLANES and lanes == V7X_LANES and npairs % PEER_GROUP == 0
    return pl.pallas_call(
        functools.partial(_peer_up_kernel, npairs=npairs),
        grid=(nhalf, t // tb),
        in_specs=[pl.BlockSpec((tb * npairs,), lambda c, i: (i,), memory_space=pltpu.SMEM),
                  pl.BlockSpec((tb, None, r, lanes), lambda c, i: (i, c, 0, 0)),
                  pl.BlockSpec((nexp, None, r, lanes), lambda c, i: (0, c, 0, 0),
                               pipeline_mode=pl.Buffered(1))],
        out_specs=pl.BlockSpec((None, tb, npairs), lambda c, i: (c, i, 0)),
        out_shape=jax.ShapeDtypeStruct((nhalf, t, npairs), F32),
        scratch_shapes=[pltpu.VMEM((npairs // PEER_GROUP, tb // V7X_SUBLANES, r, lanes), F32)],
        compiler_params=_params(("arbitrary", "arbitrary")),
        name="peer_up",
    )(eid_tok.reshape(-1), x4, utab)


def _peer_act_kernel(h_ref, g_ref, w_ref):
    h = h_ref[0] + h_ref[1]
    gelu = 0.5 * h * (1.0 + lax.erf(h * (2.0 ** -0.5)))
    w_ref[...] = g_ref[...] * gelu


def peer_act(h2, g, *, tt=512):
    _, t, npairs = h2.shape
    tt = min(tt, t)
    return pl.pallas_call(
        _peer_act_kernel,
        grid=(t // tt,),
        in_specs=[pl.BlockSpec((2, tt, npairs), lambda i: (0, i, 0)),
                  pl.BlockSpec((tt, npairs), lambda i: (i, 0))],
        out_specs=pl.BlockSpec((tt, npairs), lambda i: (i, 0)),
        out_shape=jax.ShapeDtypeStruct((t, npairs), F32),
        compiler_params=_params(("parallel",)),
        name="peer_act",
    )(h2, g)


def _peer_down_kernel(eid_ref, w_ref, tab_ref, o_ref, *, npairs):
    tb = o_ref.shape[0]
    ngroups = npairs // PEER_GROUP
    diag = _diag_mask()

    def token(t, first):
        wall = w_ref[t]
        acc = jnp.zeros((2 * V7X_SUBLANES, V7X_LANES), F32)
        for k in range(ngroups):
            g = _gather_tiles(tab_ref, eid_ref, first + k * PEER_GROUP)
            wk = jnp.broadcast_to(wall[k:k + 1, :], (V7X_SUBLANES, V7X_LANES))
            lhs = _split_bf16(jnp.where(diag, wk, 0.0))
            acc = acc + jnp.dot(lhs, g, preferred_element_type=F32)
        o_ref[t] = acc[:V7X_SUBLANES] + acc[V7X_SUBLANES:]

    _for_each_token(tb, npairs, token)


def peer_down(eid_tok, w_rep, vtab, *, tb=128):
    t, npairs = eid_tok.shape
    nexp, nhalf, r, lanes = vtab.shape
    tb = min(tb, t)
    assert r == V7X_SUBLANES and lanes == V7X_LANES and npairs // PEER_GROUP == V7X_SUBLANES
    return pl.pallas_call(
        functools.partial(_peer_down_kernel, npairs=npairs),
        grid=(nhalf, t // tb),
        in_specs=[pl.BlockSpec((tb * npairs,), lambda c, i: (i,), memory_space=pltpu.SMEM),
                  pl.BlockSpec((tb, npairs // PEER_GROUP, lanes), lambda c, i: (i, 0, 0)),
                  pl.BlockSpec((nexp, None, r, lanes), lambda c, i: (0, c, 0, 0),
                               pipeline_mode=pl.Buffered(1))],
        out_specs=pl.BlockSpec((tb, None, r, lanes), lambda c, i: (i, c, 0, 0)),
        out_shape=jax.ShapeDtypeStruct((t, nhalf, r, lanes), F32),
        compiler_params=_params(("arbitrary", "arbitrary")),
        name="peer_down",
    )(eid_tok.reshape(-1), w_rep, vtab)


def _expert_table(w):
    nexp, d = w.shape
    r = d // (2 * V7X_LANES)
    return w.astype(BF16).reshape(nexp, 2, r, V7X_LANES)


def peer_ffn(x_f32, x_bf16, wq, k1, k2, u, v):
    t, d = x_f32.shape
    r = d // (2 * V7X_LANES)
    s1, s2 = peer_scores(x_bf16, wq.astype(BF16), k1.astype(BF16), k2.astype(BF16))
    eid_tok, gate = peer_topk(s1, s2)
    npairs = eid_tok.shape[1]
    h2 = peer_up(eid_tok, x_f32.reshape(t, 2, r, V7X_LANES), _expert_table(u))
    w = peer_act(h2, gate)
    w_rep = jnp.repeat(w, V7X_SUBLANES, axis=1).reshape(t, npairs // PEER_GROUP, V7X_LANES)
    y = peer_down(eid_tok, w_rep, _expert_table(v))
    return y.reshape(t, d)


def fox_layer(x_f32, x_bf16, bsz, s, w_in, b_f, w_o, ln_g, ln_b):
    t, d = x_f32.shape
    nh = d // FOX_HEAD_DIM
    wb = w_in.astype(BF16)
    q2 = matmul(x_bf16, wb[:, :d], BF16, scale=LOG2E * FOX_HEAD_DIM ** -0.5, name="fox_q")
    k = matmul(x_bf16, wb[:, d:2 * d], BF16, name="fox_k")
    vt = matmul(wb[:, 2 * d:3 * d].T, x_bf16, BF16, b_transposed=True, name="fox_vt")
    wg = jnp.pad(wb[:, 3 * d:], ((0, 0), (0, V7X_LANES - nh)))
    gl = matmul(x_bf16, wg, F32, tn=V7X_LANES, name="fox_gate")
    bf_row = jnp.pad(b_f.astype(F32), (0, V7X_LANES - nh)).reshape(1, V7X_LANES)
    c = forget_cumsum(gl.reshape(bsz, s, V7X_LANES), bf_row)[:, :, :nh]
    tb = min(FOX_BLOCK, s)
    nb = s // tb
    hps = FOX_HEADS_PER_STEP
    c2 = c * LOG2E
    ref = c2[:, ::tb, :]
    rel = (c2.reshape(bsz, nb, tb, nh) - ref[:, :, None, :]).reshape(t, nh)
    terms, rest = [], rel
    for _ in range(FOX_BIAS_TERMS):
        part = rest.astype(BF16)
        terms.append(part)
        rest = rest - part.astype(F32)
    cterms = jnp.stack(terms, axis=-1)
    ones = jnp.ones_like(cterms)
    pad = jnp.zeros((t, nh, FOX_HEAD_DIM - 2 * FOX_BIAS_TERMS), BF16)
    qc = jnp.concatenate([cterms, ones, pad], axis=-1).reshape(bsz, s, d)
    kc = jnp.concatenate([ones, -cterms, pad], axis=-1).reshape(bsz, s, d)
    ref_rows = jnp.broadcast_to(
        jnp.transpose(ref, (0, 2, 1)).reshape(bsz, nh // hps, hps, nb).transpose(0, 1, 3, 2)[..., None],
        (bsz, nh // hps, nb, hps, tb))
    ot = fox_attention(q2.reshape(bsz, s, d), qc, k.reshape(bsz, s, d), kc, ref_rows, vt)
    return proj_residual_ln(ot, w_o.astype(BF16), x_f32, ln_g, ln_b, a_transposed=True, name="fox_out_ln")


def retention_layer(x_f32, x_bf16, bsz, s, w_in, gn_g, w_o, ln_g, ln_b):
    t, d = x_f32.shape
    nh = d // RET_QK_DIM
    qkw = 2 * nh * RET_QK_DIM
    wb = w_in.astype(BF16)
    qk = matmul(x_bf16, wb[:, :qkw], F32, name="ret_qk")
    vg = matmul(x_bf16, wb[:, qkw:], BF16, name="ret_vg")
    half = RET_QK_DIM // 2
    inv_freq = ROPE_BASE ** (-jnp.arange(half, dtype=F32) / half)
    ang = jnp.arange(s, dtype=F32)[:, None] * inv_freq[None, :]
    o = retention(qk.reshape(bsz, s, -1), vg.reshape(bsz, s, -1), jnp.cos(ang), jnp.sin(ang), gn_g)
    return proj_residual_ln(o.reshape(t, -1), w_o.astype(BF16), x_f32, ln_g, ln_b, name="ret_out_ln")


def kernel(x, l0_fox_w_in, l0_fox_b_f, l0_fox_w_o, l0_ln1_g, l0_ln1_b, l0_peer_wq, l0_peer_k1, l0_peer_k2, l0_peer_u, l0_peer_v, l0_ln2_g, l0_ln2_b, l1_ret_w_in, l1_ret_gn_g, l1_ret_w_o, l1_ln1_g, l1_ln1_b, l1_peer_wq, l1_peer_k1, l1_peer_k2, l1_peer_u, l1_peer_v, l1_ln2_g, l1_ln2_b):
    bsz, s, d = x.shape
    t = bsz * s
    xf = x.reshape(t, d).astype(F32)
    xb = xf.astype(BF16)

    xf, xb = fox_layer(xf, xb, bsz, s, l0_fox_w_in, l0_fox_b_f, l0_fox_w_o, l0_ln1_g, l0_ln1_b)
    y = peer_ffn(xf, xb, l0_peer_wq, l0_peer_k1, l0_peer_k2, l0_peer_u, l0_peer_v)
    xf, xb = residual_ln(y, xf, l0_ln2_g, l0_ln2_b, name="peer0_ln")

    xf, xb = retention_layer(xf, xb, bsz, s, l1_ret_w_in, l1_ret_gn_g, l1_ret_w_o, l1_ln1_g, l1_ln1_b)
    y = peer_ffn(xf, xb, l1_peer_wq, l1_peer_k1, l1_peer_k2, l1_peer_u, l1_peer_v)
    xf, _ = residual_ln(y, xf, l1_ln2_g, l1_ln2_b, name="peer1_ln")
    return xf.reshape(bsz, s, d).astype(x.dtype)
```
